```python
import math
import jax, jax.numpy as jnp
from jax import lax
import numpy as np

D_MODEL = 4096
BATCH = 8
SEQ = 2048
DEPTH = 2

F32 = jnp.float32
A_HEADS = D_MODEL // 256
A_DK = 128
A_DV = 128
A_WIDTH = A_HEADS * A_DV
HGRN_CHUNK = 32
B_Q_HEADS = D_MODEL // 256
B_KV_HEADS = B_Q_HEADS // 4
B_HEAD_DIM = 128
B_WIDTH = B_Q_HEADS * B_HEAD_DIM
WINDOW = 128
NUM_BUCKETS = 32
MAX_DISTANCE = 128
NEG_INF = -1e30
C_WIDTH = D_MODEL
C_GATE_BLOCKS = 16
C_BLOCK = C_WIDTH // C_GATE_BLOCKS
C_CONV = 4
RG_C = 8.0
N_EXPERTS = 32
N_GROUPS = 4
EXPERTS_PER_GROUP = N_EXPERTS // N_GROUPS
TOP_K = 2
D_EXPERT = 768
MOE_BLOCK = 128
ALPHA = (2 * DEPTH) ** 0.25
BETA = (8 * DEPTH) ** -0.25
LN_EPS = 1e-5
RMS_EPS = 1e-6
N_EVEN = (DEPTH + 1) // 2
N_ODD = DEPTH // 2
AB_SIZES = [A_HEADS * A_DK, A_HEADS * A_DK, A_HEADS * A_DK, A_WIDTH, A_WIDTH,
            B_WIDTH, B_KV_HEADS * B_HEAD_DIM, B_KV_HEADS * B_HEAD_DIM]
AB_IN = sum(AB_SIZES)
AB_SPLITS = np.cumsum(AB_SIZES)[:-1].tolist()

kernel_name = 'hybrid_hgrn2_swa_rglru_grouped_moe_encoder'


def _layer_norm(x, g, b):
    xf = x.astype(F32)
    mu = xf.mean(-1, keepdims=True)
    var = jnp.square(xf - mu).mean(-1, keepdims=True)
    return ((xf - mu) * lax.rsqrt(var + LN_EPS) * g.astype(F32) + b.astype(F32)).astype(x.dtype)


def _chunk_recurrence(q, k, v, log_f):
    B, S, H, K = q.shape
    V = v.shape[-1]
    nc = S // HGRN_CHUNK
    to_chunks = lambda t: jnp.moveaxis(t.reshape(B, nc, HGRN_CHUNK, H, t.shape[-1]), 1, 0)
    tril = jnp.tril(jnp.ones((HGRN_CHUNK, HGRN_CHUNK), bool))

    def step(state, inp):
        qc, kc, vc, lf = inp
        b = jnp.cumsum(lf, axis=1)
        b_last = b[:, -1]
        q_in = qc * jnp.exp(b)
        att = jnp.einsum('bthk,bshk->bhts', q_in, kc * jnp.exp(-b))
        att = jnp.where(tril, att, 0.0)
        o = jnp.einsum('bhts,bshv->bthv', att, vc) + jnp.einsum('bthk,bhkv->bthv', q_in, state)
        k_end = kc * jnp.exp(b_last[:, None] - b)
        state = jnp.exp(b_last)[..., None] * state + jnp.einsum('bshk,bshv->bhkv', k_end, vc)
        return state, o

    s0 = jnp.zeros((B, H, K, V), F32)
    _, o = lax.scan(step, s0, (to_chunks(q), to_chunks(k), to_chunks(v), to_chunks(log_f)))
    return jnp.moveaxis(o, 0, 1).reshape(B, S, H, V)


def _hgrn2_direction(q, f_logit, v, lb):
    f = lb + (1.0 - lb) * jax.nn.sigmoid(f_logit.astype(F32))
    return _chunk_recurrence(q.astype(F32), 1.0 - f, v.astype(F32), jnp.log(f))


def _t5_buckets(rel):
    nb = NUM_BUCKETS // 2
    ret = (rel > 0).astype(np.int32) * nb
    n = np.abs(rel)
    max_exact = nb // 2
    large = max_exact + (np.log(np.maximum(n, 1) / max_exact) / math.log(MAX_DISTANCE / max_exact)
                         * (nb - max_exact)).astype(np.int32)
    large = np.minimum(large, nb - 1)
    return ret + np.where(n < max_exact, n, large)


def _window_attention(q, k, v, sink, rel_table):
    B, S = q.shape[0], q.shape[1]
    nb = S // WINDOW
    G = B_Q_HEADS // B_KV_HEADS
    qb = q.reshape(B, nb, WINDOW, B_KV_HEADS, G, B_HEAD_DIM)

    def bands(t):
        tp = jnp.pad(t, ((0, 0), (WINDOW, WINDOW), (0, 0), (0, 0)))
        tp = tp.reshape(B, nb + 2, WINDOW, B_KV_HEADS, B_HEAD_DIM)
        return jnp.concatenate([tp[:, :-2], tp[:, 1:-1], tp[:, 2:]], axis=2)

    kw, vw = bands(k), bands(v)
    rel = np.arange(3 * WINDOW)[None, :] - WINDOW - np.arange(WINDOW)[:, None]
    kabs = np.arange(nb)[:, None] * WINDOW + np.arange(3 * WINDOW)[None, :] - WINDOW
    mask = (np.abs(rel) <= WINDOW)[None] & ((kabs >= 0) & (kabs < S))[:, None, :]
    bias = rel_table.astype(F32)[jnp.asarray(_t5_buckets(rel))]
    bias = jnp.transpose(bias, (2, 0, 1)).reshape(B_KV_HEADS, G, WINDOW, 3 * WINDOW)
    s = jnp.einsum('bnqhgd,bnkhd->bnhgqk', qb, kw).astype(F32) * (B_HEAD_DIM ** -0.5) + bias
    s = jnp.where(mask[None, :, None, None], s, NEG_INF)
    sk = sink.astype(F32).reshape(B_KV_HEADS, G)[None, None, :, :, None]
    m = jnp.maximum(s.max(-1), sk)
    p = jnp.exp(s - m[..., None])
    den = p.sum(-1) + jnp.exp(sk - m)
    p = (p / den[..., None]).astype(v.dtype)
    o = jnp.einsum('bnhgqk,bnkhd->bnqhgd', p, vw)
    return o.reshape(B, S, B_WIDTH)


def _mixer_ab(h, w_in, w_out, lb, norm_g, sink, rel_table):
    B, S, _ = h.shape
    u = h @ w_in
    q_a, f_fw, f_bw, i_a, g_a, q_b, k_b, v_b = jnp.split(u, AB_SPLITS, axis=-1)
    heads = lambda t, n: t.reshape(B, S, n, -1)
    flip = lambda t: jnp.flip(t, axis=1)
    qa, va = heads(q_a, A_HEADS), heads(i_a, A_HEADS)
    lbh = lb.reshape(A_HEADS, A_DK)
    o_a = _hgrn2_direction(qa, heads(f_fw, A_HEADS), va, lbh) + \
        flip(_hgrn2_direction(flip(qa), flip(heads(f_bw, A_HEADS)), flip(va), lbh))
    o_a = o_a * lax.rsqrt(jnp.square(o_a).mean(-1, keepdims=True) + RMS_EPS)
    o_a = (o_a.reshape(B, S, A_WIDTH) * norm_g.astype(F32) * jax.nn.silu(g_a.astype(F32))).astype(h.dtype)
    o_b = _window_attention(heads(q_b, B_Q_HEADS), heads(k_b, B_KV_HEADS), heads(v_b, B_KV_HEADS),
                            sink, rel_table)
    return jnp.concatenate([o_a, o_b], axis=-1) @ w_out


def _lin_combine(e1, e2):
    a1, b1 = e1
    a2, b2 = e2
    return a1 * a2, a2 * b1 + b2


def _rglru_direction(xc, w_r, b_r, w_i, b_i, a_param, reverse):
    B, S, _ = xc.shape
    xb = xc.reshape(B, S, C_GATE_BLOCKS, C_BLOCK)
    r = jax.nn.sigmoid((jnp.einsum('bsnc,ncd->bsnd', xb, w_r).reshape(B, S, C_WIDTH) + b_r).astype(F32))
    i = jax.nn.sigmoid((jnp.einsum('bsnc,ncd->bsnd', xb, w_i).reshape(B, S, C_WIDTH) + b_i).astype(F32))
    log_a = -RG_C * r * jax.nn.softplus(-a_param.astype(F32))
    a = jnp.exp(log_a)
    drive = jnp.sqrt(-jnp.expm1(2.0 * log_a)) * (i * xc.astype(F32))
    _, hs = lax.associative_scan(_lin_combine, (a, drive), axis=1, reverse=reverse)
    return hs


def _mixer_c(h, w_in, conv_w, conv_b, w_r, b_r, w_i, b_i, a_param, w_out):
    u = h @ w_in
    y, xr = jnp.split(u, [C_WIDTH], axis=-1)
    y = jax.nn.gelu(y)
    xc = lax.conv_general_dilated(xr, conv_w[:, None, :], window_strides=(1,),
                                  padding=[(C_CONV // 2, C_CONV - 1 - C_CONV // 2)],
                                  dimension_numbers=('NWC', 'WIO', 'NWC'),
                                  feature_group_count=C_WIDTH) + conv_b
    hs = _rglru_direction(xc, w_r[0], b_r[0], w_i[0], b_i[0], a_param[0], False) + \
        _rglru_direction(xc, w_r[1], b_r[1], w_i[1], b_i[1], a_param[1], True)
    return (hs.astype(h.dtype) * y) @ w_out


def _moe(h, router_w, router_b, w_gate, w_up, w_down):
    B, S, D = h.shape
    T = B * S
    xt = h.reshape(T, D)
    logits = jnp.dot(xt, router_w).astype(F32) + router_b.astype(F32)
    probs = jax.nn.softmax(logits, axis=-1).reshape(T, N_GROUPS, EXPERTS_PER_GROUP)
    grp = jnp.argmax(probs.max(-1), axis=-1)
    in_grp = jnp.take_along_axis(probs, grp[:, None, None], axis=1)[:, 0]
    top_p, top_i = lax.top_k(in_grp, TOP_K)
    gates = top_p / top_p.sum(-1, keepdims=True)
    experts = grp[:, None] * EXPERTS_PER_GROUP + top_i
    TK = T * TOP_K
    flat_e = experts.reshape(TK).astype(jnp.int32)
    order = jnp.argsort(flat_e)
    se = flat_e[order]
    stok = (order // TOP_K).astype(jnp.int32)
    sgate = gates.reshape(TK)[order]
    counts = jnp.bincount(flat_e, length=N_EXPERTS)
    starts = jnp.cumsum(counts) - counts
    padded = (counts + MOE_BLOCK - 1) // MOE_BLOCK * MOE_BLOCK
    pends = jnp.cumsum(padded)
    pstarts = pends - padded
    dest = pstarts[se] + jnp.arange(TK) - starts[se]
    n_blocks = -(-TK // MOE_BLOCK) + N_EXPERTS
    P = n_blocks * MOE_BLOCK
    row_tok = jnp.full((P,), T, jnp.int32).at[dest].set(stok)
    row_gate = jnp.zeros((P,), F32).at[dest].set(sgate)
    block_e = jnp.minimum(jnp.searchsorted(pends, jnp.arange(n_blocks) * MOE_BLOCK, side='right'),
                          N_EXPERTS - 1).astype(jnp.int32)

    def expert_block(args):
        toks, e = args
        xb = xt.at[toks].get(mode='fill', fill_value=0)
        hid = jax.nn.silu(xb @ w_gate[e]) * (xb @ w_up[e])
        return hid @ w_down[e]

    yb = lax.map(expert_block, (row_tok.reshape(n_blocks, MOE_BLOCK), block_e)).reshape(P, D)
    y = jnp.zeros((T, D), h.dtype).at[row_tok].add((yb * row_gate[:, None]).astype(h.dtype), mode='drop')
    return y.reshape(B, S, D)


def setup_inputs(seed: int = 0) -> dict:
    key = jax.random.key(seed)
    ks = jax.random.split(key, 24)
    nrm = lambda k, shape, scale: jax.random.normal(k, shape, F32) * scale
    u = jax.random.uniform(ks[14], (N_ODD, 2, C_WIDTH), F32, 0.9 ** (1.0 / RG_C), 0.999 ** (1.0 / RG_C))
    return {
        'x': nrm(ks[0], (BATCH, SEQ, D_MODEL), 1.0),
        'ab_w_in': nrm(ks[1], (N_EVEN, D_MODEL, AB_IN), D_MODEL ** -0.5),
        'ab_w_out': nrm(ks[2], (N_EVEN, A_WIDTH + B_WIDTH, D_MODEL), (A_WIDTH + B_WIDTH) ** -0.5 * BETA),
        'hgrn_lower_bounds': nrm(ks[3], (DEPTH + 1, A_HEADS * A_DK), 0.1),
        'hgrn_norm_g': 1.0 + nrm(ks[4], (N_EVEN, A_WIDTH), 0.02),
        'attn_sink': nrm(ks[5], (N_EVEN, B_Q_HEADS), 0.5),
        'rel_bias_table': nrm(ks[6], (NUM_BUCKETS, B_Q_HEADS), 0.1),
        'c_w_in': nrm(ks[7], (N_ODD, D_MODEL, 2 * C_WIDTH), D_MODEL ** -0.5),
        'c_conv_w': nrm(ks[8], (N_ODD, C_CONV, C_WIDTH), C_CONV ** -0.5),
        'c_conv_b': nrm(ks[9], (N_ODD, C_WIDTH), 0.02),
        'c_w_rgate': nrm(ks[10], (N_ODD, 2, C_GATE_BLOCKS, C_BLOCK, C_BLOCK), C_BLOCK ** -0.5),
        'c_b_rgate': nrm(ks[11], (N_ODD, 2, C_WIDTH), 0.02),
        'c_w_igate': nrm(ks[12], (N_ODD, 2, C_GATE_BLOCKS, C_BLOCK, C_BLOCK), C_BLOCK ** -0.5),
        'c_b_igate': nrm(ks[13], (N_ODD, 2, C_WIDTH), 0.02),
        'c_a_param': jnp.log(u) - jnp.log1p(-u),
        'c_w_out': nrm(ks[15], (N_ODD, C_WIDTH, D_MODEL), C_WIDTH ** -0.5 * BETA),
        'ln_g': 1.0 + nrm(ks[16], (DEPTH, 2, D_MODEL), 0.02),
        'ln_b': nrm(ks[17], (DEPTH, 2, D_MODEL), 0.02),
        'router_w': nrm(ks[18], (D_MODEL, N_EXPERTS), D_MODEL ** -0.5),
        'router_b': nrm(ks[19], (N_EXPERTS,), 0.01),
        'moe_w_gate': nrm(ks[20], (DEPTH, N_EXPERTS, D_MODEL, D_EXPERT), D_MODEL ** -0.5),
        'moe_w_up': nrm(ks[21], (DEPTH, N_EXPERTS, D_MODEL, D_EXPERT), D_MODEL ** -0.5),
        'moe_w_down': nrm(ks[22], (DEPTH, N_EXPERTS, D_EXPERT, D_MODEL), D_EXPERT ** -0.5 * BETA),
    }


def reference(x, ab_w_in, ab_w_out, hgrn_lower_bounds, hgrn_norm_g, attn_sink, rel_bias_table,
              c_w_in, c_conv_w, c_conv_b, c_w_rgate, c_b_rgate, c_w_igate, c_b_igate, c_a_param, c_w_out,
              ln_g, ln_b, router_w, router_b, moe_w_gate, moe_w_up, moe_w_down):
    lb_all = jnp.cumsum(jax.nn.softmax(hgrn_lower_bounds.astype(F32), axis=0), axis=0)
    h = x
    for l in range(DEPTH):
        j = l // 2
        if l % 2 == 0:
            mix = _mixer_ab(h, ab_w_in[j], ab_w_out[j], lb_all[l], hgrn_norm_g[j], attn_sink[j], rel_bias_table)
        else:
            mix = _mixer_c(h, c_w_in[j], c_conv_w[j], c_conv_b[j], c_w_rgate[j], c_b_rgate[j],
                           c_w_igate[j], c_b_igate[j], c_a_param[j], c_w_out[j])
        h = _layer_norm(ALPHA * h + mix, ln_g[l, 0], ln_b[l, 0])
        h = _layer_norm(ALPHA * h + _moe(h, router_w, router_b, moe_w_gate[l], moe_w_up[l], moe_w_down[l]),
                        ln_g[l, 1], ln_b[l, 1])
    return h
```

```python
import functools
import math

import numpy as np
import jax
import jax.numpy as jnp
from jax import lax
from jax.experimental import pallas as pl
from jax.experimental.pallas import tpu as pltpu

F32 = jnp.float32
BF16 = jnp.bfloat16
I32 = jnp.int32

D_MODEL = 4096
DEPTH = 2
A_HEADS = 16
A_DK = 128
A_WIDTH = 2048
HGRN_CHUNK = 32
HGRN_SLAB = 256
B_Q_HEADS = 16
B_KV_HEADS = 4
B_GROUP = B_Q_HEADS // B_KV_HEADS
B_HEAD_DIM = 128
B_WIDTH = 2048
WINDOW = 128
NUM_BUCKETS = 32
MAX_DISTANCE = 128
NEG_INF = -1e30
AB_IN = 13312
COL_QA, COL_FF, COL_FB, COL_IA, COL_GA = 0, 16, 32, 48, 64
COL_QB_512, COL_KB, COL_VB = 20, 96, 100
C_WIDTH = 4096
C_BLOCK = 256
C_GATE_BLOCKS = 16
C_CONV = 4
RG_C = 8.0
N_EXPERTS = 32
N_GROUPS = 4
EXPERTS_PER_GROUP = 8
TOP_K = 2
D_EXPERT = 768
MOE_BLOCK = 128
ALPHA = (2 * DEPTH) ** 0.25
LN_EPS = 1e-5
RMS_EPS = 1e-6

VMEM_LIMIT = 56 * 1024 * 1024


def _cparams(sem, vmem=VMEM_LIMIT):
    return pltpu.CompilerParams(dimension_semantics=sem, vmem_limit_bytes=vmem)


def _mm_kernel(x_ref, w_ref, o_ref):
    o_ref[...] = jnp.dot(x_ref[...], w_ref[...], preferred_element_type=F32).astype(o_ref.dtype)


def _matmul(x, w, out_dtype, tm, tn):
    m, k = x.shape
    n = w.shape[1]
    tm, tn = min(tm, m), min(tn, n)
    return pl.pallas_call(
        _mm_kernel,
        out_shape=jax.ShapeDtypeStruct((m, n), out_dtype),
        grid=(m // tm, n // tn),
        in_specs=[pl.BlockSpec((tm, k), lambda i, j: (i, 0)),
                  pl.BlockSpec((k, tn), lambda i, j: (0, j))],
        out_specs=pl.BlockSpec((tm, tn), lambda i, j: (i, j)),
        compiler_params=_cparams(("parallel", "arbitrary")),
        name="dense_matmul",
    )(x, w)


def _layer_norm_rows(z, g, b):
    mu = jnp.mean(z, axis=-1, keepdims=True)
    zc = z - mu
    var = jnp.mean(zc * zc, axis=-1, keepdims=True)
    return zc * lax.rsqrt(var + LN_EPS) * g + b


def _res_ln_kernel(h_ref, mix_ref, g_ref, b_ref, o_ref, obf_ref):
    z = ALPHA * h_ref[...] + mix_ref[...]
    out = _layer_norm_rows(z, g_ref[...], b_ref[...])
    o_ref[...] = out
    obf_ref[...] = out.astype(BF16)


def _res_ln(h, mix, g, b, tb=256):
    t, d = h.shape
    tb = min(tb, t)
    row = pl.BlockSpec((tb, d), lambda i: (i, 0))
    vec = pl.BlockSpec((1, d), lambda i: (0, 0))
    return pl.pallas_call(
        _res_ln_kernel,
        out_shape=(jax.ShapeDtypeStruct((t, d), F32), jax.ShapeDtypeStruct((t, d), BF16)),
        grid=(t // tb,),
        in_specs=[row, row, vec, vec],
        out_specs=(row, row),
        compiler_params=_cparams(("parallel",)),
        name="residual_layernorm",
    )(h, mix, g.reshape(1, d), b.reshape(1, d))


def _split_dot(mat_bf, x):
    hi = x.astype(BF16)
    lo = (x - hi.astype(F32)).astype(BF16)
    return (jnp.dot(mat_bf, hi, preferred_element_type=F32)
            + jnp.dot(mat_bf, lo, preferred_element_type=F32))


def _hgrn_kernel(q_ref, ff_ref, fb_ref, v_ref, g_ref, lb_ref, ng_ref, o_ref, acc_ref, *, n_slabs):
    slab, chunk = HGRN_SLAB, HGRN_CHUNK
    n_chunks = slab // chunk
    row = lax.broadcasted_iota(I32, (slab, slab), 0)
    col = lax.broadcasted_iota(I32, (slab, slab), 1)
    same_chunk = (row // chunk) == (col // chunk)
    lb = lb_ref[0]

    def direction(f_ref, reverse):
        keep = same_chunk & ((col >= row) if reverse else (col <= row))
        keep_strict = same_chunk & ((col < row) if reverse else (col > row))
        incl = jnp.where(keep, 1.0, 0.0).astype(BF16)
        rest = jnp.where(keep_strict, 1.0, 0.0).astype(BF16)

        def slab_body(i, state_t):
            si = (n_slabs - 1 - i) if reverse else i
            r0 = pl.multiple_of(si * slab, slab)
            q = q_ref[pl.ds(r0, slab), :]
            v = v_ref[pl.ds(r0, slab), :]
            f = lb + (1.0 - lb) * jax.nn.sigmoid(f_ref[pl.ds(r0, slab), :])
            lf = jnp.log(f)
            kk = 1.0 - f
            b = _split_dot(incl, lf)
            c = _split_dot(rest, lf)
            q_in = q * jnp.exp(b)
            k_dec = (kk * jnp.exp(-b)).astype(BF16)
            k_end = (kk * jnp.exp(c)).astype(BF16)
            q_bf = q_in.astype(BF16)
            v_bf = v.astype(BF16)
            att = lax.dot_general(q_bf, k_dec, (((1,), (1,)), ((), ())), preferred_element_type=F32)
            att = jnp.where(keep, att, 0.0).astype(BF16)
            o_intra = jnp.dot(att, v_bf, preferred_element_type=F32)
            order = range(n_chunks - 1, -1, -1) if reverse else range(n_chunks)
            for j in order:
                lo, hi = j * chunk, (j + 1) * chunk
                o_j = o_intra[lo:hi] + lax.dot_general(
                    q_bf[lo:hi], state_t.astype(BF16), (((1,), (1,)), ((), ())),
                    preferred_element_type=F32)
                last = lo if reverse else hi - 1
                decay = jnp.exp(b[last:last + 1, :])
                upd = lax.dot_general(v_bf[lo:hi], k_end[lo:hi], (((0,), (0,)), ((), ())),
                                      preferred_element_type=F32)
                state_t = state_t * decay + upd
                rows = pl.ds(r0 + lo, chunk)
                if reverse:
                    acc_ref[rows, :] = acc_ref[rows, :] + o_j
                else:
                    acc_ref[rows, :] = o_j
            return state_t

        lax.fori_loop(0, n_slabs, slab_body, jnp.zeros((A_DK, A_DK), F32))

    direction(ff_ref, False)
    direction(fb_ref, True)

    ng = ng_ref[0]

    def finish(i, carry):
        r0 = pl.multiple_of(i * slab, slab)
        o = acc_ref[pl.ds(r0, slab), :]
        o = o * lax.rsqrt(jnp.mean(o * o, axis=-1, keepdims=True) + RMS_EPS)
        g = g_ref[pl.ds(r0, slab), :]
        o_ref[pl.ds(r0, slab), :] = (o * ng * (g * jax.nn.sigmoid(g))).astype(o_ref.dtype)
        return carry

    lax.fori_loop(0, n_slabs, finish, 0)


def _hgrn(u2d, lb, norm_g, seq, batch):
    ncol = AB_IN // 128
    blk = lambda off: pl.BlockSpec((seq, 128), lambda b, h: (0, b * ncol + off + h))
    par = pl.BlockSpec((1, 1, 128), lambda b, h: (h, 0, 0))
    return pl.pallas_call(
        functools.partial(_hgrn_kernel, n_slabs=seq // HGRN_SLAB),
        out_shape=jax.ShapeDtypeStruct((seq, batch * A_WIDTH), BF16),
        grid=(batch, A_HEADS),
        in_specs=[blk(COL_QA), blk(COL_FF), blk(COL_FB), blk(COL_IA), blk(COL_GA), par, par],
        out_specs=pl.BlockSpec((seq, 128), lambda b, h: (0, b * A_HEADS + h)),
        scratch_shapes=[pltpu.VMEM((seq, 128), F32)],
        compiler_params=_cparams(("parallel", "parallel")),
        name="hgrn2_bidirectional",
    )(u2d, u2d, u2d, u2d, u2d, lb.reshape(A_HEADS, 1, A_DK), norm_g.reshape(A_HEADS, 1, 128))


def _t5_buckets(rel):
    nb = NUM_BUCKETS // 2
    ret = (rel > 0).astype(np.int32) * nb
    n = np.abs(rel)
    max_exact = nb // 2
    large = max_exact + (np.log(np.maximum(n, 1) / max_exact) / math.log(MAX_DISTANCE / max_exact)
                         * (nb - max_exact)).astype(np.int32)
    large = np.minimum(large, nb - 1)
    return ret + np.where(n < max_exact, n, large)


def _attn_kernel(q_ref, k_ref, v_ref, bias_ref, sink_ref, o_ref, kpad_ref, vpad_ref, *, seq):
    w = WINDOW
    nblk = seq // w
    zeros = jnp.zeros((w, B_HEAD_DIM), BF16)
    kpad_ref[0:w, :] = zeros
    vpad_ref[0:w, :] = zeros
    kpad_ref[w + seq:2 * w + seq, :] = zeros
    vpad_ref[w + seq:2 * w + seq, :] = zeros
    kpad_ref[w:w + seq, :] = k_ref[...].astype(BF16)
    vpad_ref[w:w + seq, :] = v_ref[...].astype(BF16)
    qi = lax.broadcasted_iota(I32, (w, 3 * w), 0)
    kj = lax.broadcasted_iota(I32, (w, 3 * w), 1)
    rel = kj - w - qi
    in_window = (rel <= w) & (rel >= -w)
    scale = B_HEAD_DIM ** -0.5

    def block(n, carry):
        r0 = pl.multiple_of(n * w, w)
        kabs = kj + (n - 1) * w
        mask = in_window & (kabs >= 0) & (kabs < seq)
        kb = kpad_ref[pl.ds(r0, 3 * w), :]
        vb = vpad_ref[pl.ds(r0, 3 * w), :]
        for g in range(B_GROUP):
            q = q_ref[pl.ds(r0, w), g * B_HEAD_DIM:(g + 1) * B_HEAD_DIM].astype(BF16)
            s = lax.dot_general(q, kb, (((1,), (1,)), ((), ())), preferred_element_type=F32)
            s = s * scale + bias_ref[g]
            s = jnp.where(mask, s, NEG_INF)
            sk = sink_ref[g][:, 0:1]
            m = jnp.maximum(jnp.max(s, axis=-1, keepdims=True), sk)
            p = jnp.exp(s - m)
            den = jnp.sum(p, axis=-1, keepdims=True) + jnp.exp(sk - m)
            o = jnp.dot(p.astype(BF16), vb, preferred_element_type=F32) / den
            o_ref[pl.ds(r0, w), g * B_HEAD_DIM:(g + 1) * B_HEAD_DIM] = o.astype(o_ref.dtype)
        return carry

    lax.fori_loop(0, nblk, block, 0)


def _window_attention(u2d, sink, rel_table, seq, batch):
    rel = np.arange(3 * WINDOW)[None, :] - WINDOW - np.arange(WINDOW)[:, None]
    bias = rel_table.astype(F32)[jnp.asarray(_t5_buckets(rel))]
    bias = jnp.transpose(bias, (2, 0, 1))
    sink_b = jnp.broadcast_to(sink.astype(F32)[:, None, None], (B_Q_HEADS, 1, 128))
    ncol = AB_IN // 128
    qw = B_GROUP * B_HEAD_DIM
    return pl.pallas_call(
        functools.partial(_attn_kernel, seq=seq),
        out_shape=jax.ShapeDtypeStruct((seq, batch * B_WIDTH), BF16),
        grid=(batch, B_KV_HEADS),
        in_specs=[pl.BlockSpec((seq, qw), lambda b, h: (0, b * (AB_IN // qw) + COL_QB_512 + h)),
                  pl.BlockSpec((seq, 128), lambda b, h: (0, b * ncol + COL_KB + h)),
                  pl.BlockSpec((seq, 128), lambda b, h: (0, b * ncol + COL_VB + h)),
                  pl.BlockSpec((B_GROUP, WINDOW, 3 * WINDOW), lambda b, h: (h, 0, 0)),
                  pl.BlockSpec((B_GROUP, 1, 128), lambda b, h: (h, 0, 0))],
        out_specs=pl.BlockSpec((seq, qw), lambda b, h: (0, b * B_KV_HEADS + h)),
        scratch_shapes=[pltpu.VMEM((seq + 2 * WINDOW, B_HEAD_DIM), BF16),
                        pltpu.VMEM((seq + 2 * WINDOW, B_HEAD_DIM), BF16)],
        compiler_params=_cparams(("parallel", "parallel")),
        name="window_attention",
    )(u2d, u2d, u2d, bias, sink_b)


def _gelu_tanh(x):
    return 0.5 * x * (1.0 + jnp.tanh(math.sqrt(2.0 / math.pi) * (x + 0.044715 * x * x * x)))


def _rglru_kernel(xr_ref, prev_ref, next_ref, y_ref, cw_ref, cb_ref, wr_ref, br_ref, wi_ref, bi_ref,
                  sp_ref, o_ref, hsf_ref, a_ref, d_ref, hs_ref, h_ref, *, ts, n_t, batch):
    p = pl.program_id(1)
    t = pl.program_id(2)
    tb = t + p * (n_t - 1 - 2 * t)

    @pl.when(t == 0)
    def _():
        h_ref[...] = jnp.zeros_like(h_ref)

    xr = xr_ref[...]
    prev = jnp.where(tb == 0, 0.0, prev_ref[...])
    nxt = jnp.where(tb == n_t - 1, 0.0, next_ref[...])
    xfull = jnp.concatenate([prev, xr, nxt], axis=0)
    cw = cw_ref[...]
    xc = cb_ref[...].reshape(1, 1, C_BLOCK) + sum(
        xfull[j:j + ts] * cw[j:j + 1, :].reshape(1, 1, C_BLOCK) for j in range(C_CONV))
    x2 = xc.reshape(ts * batch, C_BLOCK)
    xb = x2.astype(BF16)
    r = jax.nn.sigmoid(jnp.dot(xb, wr_ref[...], preferred_element_type=F32) + br_ref[0])
    gi = jax.nn.sigmoid(jnp.dot(xb, wi_ref[...], preferred_element_type=F32) + bi_ref[0])
    log_a = (-RG_C) * r * sp_ref[0]
    a = jnp.exp(log_a)
    drive = jnp.sqrt(1.0 - a * a) * (gi * x2)
    a_ref[...] = a.reshape(ts, batch, C_BLOCK)
    d_ref[...] = drive.reshape(ts, batch, C_BLOCK)

    def step(i, h):
        idx = i + p * (ts - 1 - 2 * i)
        h = a_ref[idx] * h + d_ref[idx]
        hs_ref[idx] = h
        return h

    h_ref[...] = lax.fori_loop(0, ts, step, h_ref[...], unroll=8)
    base = pl.multiple_of(tb * ts, ts)

    @pl.when(p == 0)
    def _():
        hsf_ref[pl.ds(base, ts)] = hs_ref[...]

    @pl.when(p == 1)
    def _():
        hs = hsf_ref[pl.ds(base, ts)] + hs_ref[...]
        o_ref[...] = (hs.reshape(ts * batch, C_BLOCK) * _gelu_tanh(y_ref[...])).astype(o_ref.dtype)


def _rglru(u, conv_w, conv_b, w_r, b_r, w_i, b_i, a_param, seq, batch):
    ts = min(256, seq)
    n_t = seq // ts
    nc = C_GATE_BLOCKS
    u3 = u.reshape(seq, batch, 2 * C_WIDTH)
    sp = jax.nn.softplus(-a_param.astype(F32)).reshape(2, nc, 1, C_BLOCK)
    tblk = lambda p, t: t + p * (n_t - 1 - 2 * t)
    pinned = lambda p, t: p * tblk(p, t) + (1 - p) * (n_t - 1)
    per_dir = lambda shape: pl.BlockSpec((None, None) + shape, lambda c, p, t: (p, c, 0, 0))
    kern = functools.partial(_rglru_kernel, ts=ts, n_t=n_t, batch=batch)
    return pl.pallas_call(
        kern,
        out_shape=jax.ShapeDtypeStruct((seq * batch, C_WIDTH), BF16),
        grid=(nc, 2, n_t),
        in_specs=[
            pl.BlockSpec((ts, batch, C_BLOCK), lambda c, p, t: (tblk(p, t), 0, nc + c)),
            pl.BlockSpec((2, batch, C_BLOCK),
                         lambda c, p, t: (jnp.maximum(tblk(p, t) * (ts // 2) - 1, 0), 0, nc + c)),
            pl.BlockSpec((1, batch, C_BLOCK),
                         lambda c, p, t: (jnp.minimum((tblk(p, t) + 1) * ts, seq - 1), 0, nc + c)),
            pl.BlockSpec((ts * batch, C_BLOCK), lambda c, p, t: (pinned(p, t), c)),
            pl.BlockSpec((C_CONV, C_BLOCK), lambda c, p, t: (0, c)),
            pl.BlockSpec((1, C_BLOCK), lambda c, p, t: (0, c)),
            per_dir((C_BLOCK, C_BLOCK)), per_dir((1, C_BLOCK)),
            per_dir((C_BLOCK, C_BLOCK)), per_dir((1, C_BLOCK)),
            per_dir((1, C_BLOCK)),
        ],
        out_specs=pl.BlockSpec((ts * batch, C_BLOCK), lambda c, p, t: (pinned(p, t), c)),
        scratch_shapes=[pltpu.VMEM((seq, batch, C_BLOCK), F32),
                        pltpu.VMEM((ts, batch, C_BLOCK), F32),
                        pltpu.VMEM((ts, batch, C_BLOCK), F32),
                        pltpu.VMEM((ts, batch, C_BLOCK), F32),
                        pltpu.VMEM((batch, C_BLOCK), F32)],
        compiler_params=_cparams(("parallel", "arbitrary", "arbitrary")),
        name="rglru_bidirectional",
    )(u3, u3, u3, u, conv_w, conv_b.reshape(1, C_WIDTH),
      w_r.astype(BF16), b_r.reshape(2, nc, 1, C_BLOCK), w_i.astype(BF16), b_i.reshape(2, nc, 1, C_BLOCK), sp)


def _route_kernel(h_ref, rwt_ref, rb_ref, e1_ref, e2_ref, g1_ref, g2_ref, r1_ref, r2_ref, cnt_ref,
                  carry_ref, *, tb):
    @pl.when(pl.program_id(0) == 0)
    def _():
        carry_ref[...] = jnp.zeros_like(carry_ref)

    logits = lax.dot_general(rwt_ref[...], h_ref[...], (((1,), (1,)), ((), ())),
                             precision=lax.Precision.HIGHEST,
                             preferred_element_type=F32) + rb_ref[...]
    mx = jnp.max(logits, axis=0, keepdims=True)
    ex = jnp.exp(logits - mx)
    probs = ex / jnp.sum(ex, axis=0, keepdims=True)
    eid = lax.broadcasted_iota(I32, (N_EXPERTS, tb), 0)
    gmax = jnp.max(probs.reshape(N_GROUPS, EXPERTS_PER_GROUP, tb), axis=1)
    gid = lax.broadcasted_iota(I32, (N_GROUPS, tb), 0)
    grp = jnp.min(jnp.where(gmax == jnp.max(gmax, axis=0, keepdims=True), gid, N_GROUPS),
                  axis=0, keepdims=True)
    sel = jnp.where((eid // EXPERTS_PER_GROUP) == grp, probs, -1.0)
    m1 = jnp.max(sel, axis=0, keepdims=True)
    e1 = jnp.min(jnp.where(sel == m1, eid, N_EXPERTS), axis=0, keepdims=True)
    sel2 = jnp.where(eid == e1, -1.0, sel)
    m2 = jnp.max(sel2, axis=0, keepdims=True)
    e2 = jnp.min(jnp.where(sel2 == m2, eid, N_EXPERTS), axis=0, keepdims=True)
    tot = m1 + m2
    e1_ref[...] = e1
    e2_ref[...] = e2
    g1_ref[...] = m1 / tot
    g2_ref[...] = m2 / tot
    oh1 = eid == e1
    oh2 = eid == e2
    oh = jnp.where(oh1 | oh2, 1.0, 0.0)
    before = jnp.where(lax.broadcasted_iota(I32, (tb, tb), 0) < lax.broadcasted_iota(I32, (tb, tb), 1),
                       1.0, 0.0).astype(BF16)
    base = carry_ref[...] + jnp.dot(oh.astype(BF16), before, preferred_element_type=F32)
    r1_ref[...] = jnp.sum(jnp.where(oh1, base, 0.0), axis=0, keepdims=True).astype(I32)
    r2_ref[...] = jnp.sum(jnp.where(oh2, base, 0.0), axis=0, keepdims=True).astype(I32)
    carry_ref[...] = carry_ref[...] + jnp.sum(oh, axis=1, keepdims=True)
    cnt_ref[...] = jnp.broadcast_to(carry_ref[...], cnt_ref.shape).astype(I32)


def _route(h, router_w, router_b, tb=512):
    t, d = h.shape
    tb = min(tb, t)
    lane = lambda dt: jax.ShapeDtypeStruct((1, t), dt)
    lspec = pl.BlockSpec((1, tb), lambda i: (0, i))
    outs = pl.pallas_call(
        functools.partial(_route_kernel, tb=tb),
        out_shape=(lane(I32), lane(I32), lane(F32), lane(F32), lane(I32), lane(I32),
                   jax.ShapeDtypeStruct((N_EXPERTS, 128), I32)),
        grid=(t // tb,),
        in_specs=[pl.BlockSpec((tb, d), lambda i: (i, 0)),
                  pl.BlockSpec((N_EXPERTS, d), lambda i: (0, 0)),
                  pl.BlockSpec((N_EXPERTS, 1), lambda i: (0, 0))],
        out_specs=(lspec, lspec, lspec, lspec, lspec, lspec,
                   pl.BlockSpec((N_EXPERTS, 128), lambda i: (0, 0))),
        scratch_shapes=[pltpu.VMEM((N_EXPERTS, 1), F32)],
        compiler_params=_cparams(("arbitrary",)),
        name="moe_route",
    )(h, router_w.T, router_b.reshape(N_EXPERTS, 1).astype(F32))
    e1, e2, g1, g2, r1, r2, cnt = outs
    return e1[0], e2[0], g1[0], g2[0], r1[0], r2[0], cnt[:, 0]


def _row_copy(src_ref, src_row, dst_ref, dst_row, sem):
    return pltpu.make_async_copy(src_ref.at[pl.ds(src_row, 1)], dst_ref.at[pl.ds(dst_row, 1)], sem)


def _dispatch_kernel(d1_ref, d2_ref, h_ref, xs_in_ref, xs_ref, sems, *, tb):
    del xs_in_ref
    t0 = pl.program_id(0) * tb

    def start(r, carry):
        _row_copy(h_ref, r, xs_ref, d1_ref[t0 + r], sems.at[0]).start()
        _row_copy(h_ref, r, xs_ref, d2_ref[t0 + r], sems.at[1]).start()
        return carry

    def wait(r, carry):
        _row_copy(h_ref, r, xs_ref, d1_ref[t0 + r], sems.at[0]).wait()
        _row_copy(h_ref, r, xs_ref, d2_ref[t0 + r], sems.at[1]).wait()
        return carry

    lax.fori_loop(0, tb, start, 0)
    lax.fori_loop(0, tb, wait, 0)


def _dispatch(h, dest1, dest2, n_rows, tb=128):
    t, d = h.shape
    tb = min(tb, t)
    xs0 = jnp.zeros((n_rows, d), F32)
    return pl.pallas_call(
        functools.partial(_dispatch_kernel, tb=tb),
        out_shape=jax.ShapeDtypeStruct((n_rows, d), F32),
        grid_spec=pltpu.PrefetchScalarGridSpec(
            num_scalar_prefetch=2,
            grid=(t // tb,),
            in_specs=[pl.BlockSpec((tb, d), lambda i, d1, d2: (i, 0)),
                      pl.BlockSpec(memory_space=pl.ANY)],
            out_specs=pl.BlockSpec(memory_space=pl.ANY),
            scratch_shapes=[pltpu.SemaphoreType.DMA((2,))],
        ),
        input_output_aliases={3: 0},
        compiler_params=_cparams(("arbitrary",)),
        name="moe_dispatch",
    )(dest1, dest2, h, xs0)


def _expert_kernel(be_ref, nu_ref, x_ref, wg_ref, wu_ref, wd_ref, o_ref):
    b = pl.program_id(0)

    @pl.when(b < nu_ref[0])
    def _():
        x = x_ref[...].astype(BF16)
        hg = jnp.dot(x, wg_ref[...], preferred_element_type=F32)
        hu = jnp.dot(x, wu_ref[...], preferred_element_type=F32)
        hid = (hg * jax.nn.sigmoid(hg) * hu).astype(BF16)
        o_ref[...] = jnp.dot(hid, wd_ref[...], preferred_element_type=F32)

    @pl.when(b >= nu_ref[0])
    def _():
        o_ref[...] = jnp.zeros_like(o_ref)


def _experts(xs, block_e, n_used, w_gate, w_up, w_down):
    p, d = xs.shape
    nblk = p // MOE_BLOCK
    de = w_gate.shape[-1]
    xrow = lambda b, be, nu: (jnp.minimum(b, jnp.maximum(nu[0] - 1, 0)), 0)
    return pl.pallas_call(
        _expert_kernel,
        out_shape=jax.ShapeDtypeStruct((p, d), F32),
        grid_spec=pltpu.PrefetchScalarGridSpec(
            num_scalar_prefetch=2,
            grid=(nblk,),
            in_specs=[pl.BlockSpec((MOE_BLOCK, d), xrow),
                      pl.BlockSpec((None, d, de), lambda b, be, nu: (be[b], 0, 0)),
                      pl.BlockSpec((None, d, de), lambda b, be, nu: (be[b], 0, 0)),
                      pl.BlockSpec((None, de, d), lambda b, be, nu: (be[b], 0, 0))],
            out_specs=pl.BlockSpec((MOE_BLOCK, d), lambda b, be, nu: (b, 0)),
        ),
        compiler_params=_cparams(("arbitrary",)),
        name="moe_experts",
    )(block_e, n_used, xs, w_gate, w_up, w_down)


def _combine_ln_kernel(d1_ref, d2_ref, yb_ref, h_ref, g1_ref, g2_ref, lg_ref, lb_ref, o_ref, obf_ref,
                       y1_ref, y2_ref, sems, *, tb):
    t0 = pl.program_id(0) * tb

    def start(r, carry):
        _row_copy(yb_ref, d1_ref[t0 + r], y1_ref, r, sems.at[0]).start()
        _row_copy(yb_ref, d2_ref[t0 + r], y2_ref, r, sems.at[1]).start()
        return carry

    def wait(r, carry):
        _row_copy(yb_ref, d1_ref[t0 + r], y1_ref, r, sems.at[0]).wait()
        _row_copy(yb_ref, d2_ref[t0 + r], y2_ref, r, sems.at[1]).wait()
        return carry

    lax.fori_loop(0, tb, start, 0)
    lax.fori_loop(0, tb, wait, 0)
    y = y1_ref[...] * g1_ref[...] + y2_ref[...] * g2_ref[...]
    out = _layer_norm_rows(ALPHA * h_ref[...] + y, lg_ref[...], lb_ref[...])
    o_ref[...] = out
    obf_ref[...] = out.astype(BF16)


def _combine_ln(yb, h, dest1, dest2, g1, g2, ln_g, ln_b, tb=128):
    t, d = h.shape
    tb = min(tb, t)
    row = pl.BlockSpec((tb, d), lambda i, d1, d2: (i, 0))
    colv = pl.BlockSpec((tb, 1), lambda i, d1, d2: (i, 0))
    vec = pl.BlockSpec((1, d), lambda i, d1, d2: (0, 0))
    return pl.pallas_call(
        functools.partial(_combine_ln_kernel, tb=tb),
        out_shape=(jax.ShapeDtypeStruct((t, d), F32), jax.ShapeDtypeStruct((t, d), BF16)),
        grid_spec=pltpu.PrefetchScalarGridSpec(
            num_scalar_prefetch=2,
            grid=(t // tb,),
            in_specs=[pl.BlockSpec(memory_space=pl.ANY), row, colv, colv, vec, vec],
            out_specs=(row, row),
            scratch_shapes=[pltpu.VMEM((tb, d), F32), pltpu.VMEM((tb, d), F32),
                            pltpu.SemaphoreType.DMA((2,))],
        ),
        compiler_params=_cparams(("arbitrary",)),
        name="moe_combine_layernorm",
    )(dest1, dest2, yb, h, g1.reshape(t, 1), g2.reshape(t, 1), ln_g.reshape(1, d), ln_b.reshape(1, d))


def _moe_ln(h, router_w, router_b, w_gate, w_up, w_down, ln_g, ln_b):
    t, d = h.shape
    e1, e2, g1, g2, r1, r2, counts = _route(h, router_w, router_b)
    padded = (counts + MOE_BLOCK - 1) // MOE_BLOCK * MOE_BLOCK
    pends = jnp.cumsum(padded)
    pstarts = pends - padded
    dest1 = pstarts[e1] + r1
    dest2 = pstarts[e2] + r2
    nblk = -(-(t * TOP_K) // MOE_BLOCK) + N_EXPERTS
    n_used = (pends[-1] // MOE_BLOCK).astype(I32)
    blk_row = jnp.minimum(jnp.arange(nblk, dtype=I32), jnp.maximum(n_used - 1, 0)) * MOE_BLOCK
    block_e = jnp.minimum(jnp.searchsorted(pends, blk_row, side='right'), N_EXPERTS - 1).astype(I32)
    xs = _dispatch(h, dest1, dest2, nblk * MOE_BLOCK)
    yb = _experts(xs, block_e, n_used.reshape(1), w_gate, w_up, w_down)
    return _combine_ln(yb, h, dest1, dest2, g1, g2, ln_g, ln_b)


def kernel(x, ab_w_in, ab_w_out, hgrn_lower_bounds, hgrn_norm_g, attn_sink, rel_bias_table, c_w_in, c_conv_w, c_conv_b, c_w_rgate, c_b_rgate, c_w_igate, c_b_igate, c_a_param, c_w_out, ln_g, ln_b, router_w, router_b, moe_w_gate, moe_w_up, moe_w_down):
    batch, seq, d = x.shape
    t = batch * seq
    lb_all = jnp.cumsum(jax.nn.softmax(hgrn_lower_bounds.astype(F32), axis=0), axis=0)
    h = jnp.transpose(x, (1, 0, 2)).reshape(t, d)
    h_bf = h.astype(BF16)
    for l in range(DEPTH):
        j = l // 2
        if l % 2 == 0:
            u = _matmul(h_bf, ab_w_in[j].astype(BF16), F32, 1024, 1024)
            u2d = u.reshape(seq, batch * AB_IN)
            o_a = _hgrn(u2d, lb_all[l], hgrn_norm_g[j], seq, batch)
            o_b = _window_attention(u2d, attn_sink[j], rel_bias_table, seq, batch)
            o_ab = jnp.concatenate([o_a.reshape(t, A_WIDTH), o_b.reshape(t, B_WIDTH)], axis=-1)
            mix = _matmul(o_ab, ab_w_out[j].astype(BF16), F32, 1024, 1024)
        else:
            u = _matmul(h_bf, c_w_in[j].astype(BF16), F32, 1024, 1024)
            z = _rglru(u, c_conv_w[j], c_conv_b[j], c_w_rgate[j], c_b_rgate[j], c_w_igate[j], c_b_igate[j],
                       c_a_param[j], seq, batch)
            mix = _matmul(z, c_w_out[j].astype(BF16), F32, 1024, 1024)
        h, _ = _res_ln(h, mix, ln_g[l, 0], ln_b[l, 0])
        h, h_bf = _moe_ln(h, router_w, router_b, moe_w_gate[l].astype(BF16), moe_w_up[l].astype(BF16),
                          moe_w_down[l].astype(BF16), ln_g[l, 1], ln_b[l, 1])
    return jnp.transpose(h.reshape(seq, batch, d), (1, 0, 2))
```

```python
import functools
import math

import numpy as np
import jax
import jax.numpy as jnp
from jax import lax
from jax.experimental import pallas as pl
from jax.experimental.pallas import tpu as pltpu

F32 = jnp.float32
BF16 = jnp.bfloat16
I32 = jnp.int32

D_MODEL = 4096
DEPTH = 2
A_HEADS = 16
A_DK = 128
A_WIDTH = 2048
HGRN_CHUNK = 32
HGRN_SLAB = 256
B_Q_HEADS = 16
B_KV_HEADS = 4
B_GROUP = B_Q_HEADS // B_KV_HEADS
B_HEAD_DIM = 128
B_WIDTH = 2048
WINDOW = 128
NUM_BUCKETS = 32
MAX_DISTANCE = 128
NEG_INF = -1e30
AB_IN = 13312
COL_QA, COL_FF, COL_FB, COL_IA, COL_GA = 0, 16, 32, 48, 64
COL_QB_512, COL_KB, COL_VB = 20, 96, 100
C_WIDTH = 4096
C_BLOCK = 256
C_GATE_BLOCKS = 16
C_CONV = 4
RG_C = 8.0
N_EXPERTS = 32
N_GROUPS = 4
EXPERTS_PER_GROUP = 8
TOP_K = 2
D_EXPERT = 768
MOE_BLOCK = 128
ALPHA = (2 * DEPTH) ** 0.25
LN_EPS = 1e-5
RMS_EPS = 1e-6

VMEM_LIMIT = 56 * 1024 * 1024


def _cparams(sem, vmem=VMEM_LIMIT):
    return pltpu.CompilerParams(dimension_semantics=sem, vmem_limit_bytes=vmem)


def _mm_kernel(x_ref, w_ref, o_ref):
    o_ref[...] = jnp.dot(x_ref[...], w_ref[...], preferred_element_type=F32).astype(o_ref.dtype)


def _matmul(x, w, out_dtype, tm, tn):
    m, k = x.shape
    n = w.shape[1]
    tm, tn = min(tm, m), min(tn, n)
    return pl.pallas_call(
        _mm_kernel,
        out_shape=jax.ShapeDtypeStruct((m, n), out_dtype),
        grid=(m // tm, n // tn),
        in_specs=[pl.BlockSpec((tm, k), lambda i, j: (i, 0)),
                  pl.BlockSpec((k, tn), lambda i, j: (0, j))],
        out_specs=pl.BlockSpec((tm, tn), lambda i, j: (i, j)),
        compiler_params=_cparams(("parallel", "arbitrary")),
        name="dense_matmul",
    )(x, w)


def _mm2_kernel(xa_ref, xb_ref, w_ref, o_ref, *, ka):
    acc = jnp.dot(xa_ref[...], w_ref[0:ka, :], preferred_element_type=F32)
    acc = acc + jnp.dot(xb_ref[...], w_ref[ka:, :], preferred_element_type=F32)
    o_ref[...] = acc.astype(o_ref.dtype)


def _matmul2(xa, xb, w, out_dtype, tm, tn):
    m, ka = xa.shape
    kb = xb.shape[1]
    n = w.shape[1]
    tm, tn = min(tm, m), min(tn, n)
    return pl.pallas_call(
        functools.partial(_mm2_kernel, ka=ka),
        out_shape=jax.ShapeDtypeStruct((m, n), out_dtype),
        grid=(m // tm, n // tn),
        in_specs=[pl.BlockSpec((tm, ka), lambda i, j: (i, 0)),
                  pl.BlockSpec((tm, kb), lambda i, j: (i, 0)),
                  pl.BlockSpec((ka + kb, tn), lambda i, j: (0, j))],
        out_specs=pl.BlockSpec((tm, tn), lambda i, j: (i, j)),
        compiler_params=_cparams(("parallel", "arbitrary")),
        name="dense_matmul_2in",
    )(xa, xb, w)


def _layer_norm_rows(z, g, b):
    mu = jnp.mean(z, axis=-1, keepdims=True)
    zc = z - mu
    var = jnp.mean(zc * zc, axis=-1, keepdims=True)
    return zc * lax.rsqrt(var + LN_EPS) * g + b


def _res_ln_kernel(h_ref, mix_ref, g_ref, b_ref, o_ref, obf_ref):
    z = ALPHA * h_ref[...] + mix_ref[...]
    out = _layer_norm_rows(z, g_ref[...], b_ref[...])
    o_ref[...] = out
    obf_ref[...] = out.astype(BF16)


def _res_ln(h, mix, g, b, tb=256):
    t, d = h.shape
    tb = min(tb, t)
    row = pl.BlockSpec((tb, d), lambda i: (i, 0))
    vec = pl.BlockSpec((1, d), lambda i: (0, 0))
    return pl.pallas_call(
        _res_ln_kernel,
        out_shape=(jax.ShapeDtypeStruct((t, d), F32), jax.ShapeDtypeStruct((t, d), BF16)),
        grid=(t // tb,),
        in_specs=[row, row, vec, vec],
        out_specs=(row, row),
        compiler_params=_cparams(("parallel",)),
        name="residual_layernorm",
    )(h, mix, g.reshape(1, d), b.reshape(1, d))


def _split_dot(mat_bf, x):
    hi = x.astype(BF16)
    lo = (x - hi.astype(F32)).astype(BF16)
    return (jnp.dot(mat_bf, hi, preferred_element_type=F32)
            + jnp.dot(mat_bf, lo, preferred_element_type=F32))


def _hgrn_kernel(q_ref, ff_ref, fb_ref, v_ref, g_ref, lb_ref, ng_ref, o_ref, accf_ref, accb_ref, *, n_slabs):
    slab, chunk = HGRN_SLAB, HGRN_CHUNK
    n_chunks = slab // chunk
    row = lax.broadcasted_iota(I32, (slab, slab), 0)
    col = lax.broadcasted_iota(I32, (slab, slab), 1)
    same_chunk = (row // chunk) == (col // chunk)
    lb = lb_ref[0]

    def masks(reverse):
        keep = same_chunk & ((col >= row) if reverse else (col <= row))
        keep_strict = same_chunk & ((col < row) if reverse else (col > row))
        incl = jnp.where(keep, 1.0, 0.0).astype(BF16)
        rest = jnp.where(keep_strict, 1.0, 0.0).astype(BF16)
        return keep, incl, rest

    def slab_pass(f_ref, acc_ref, si, state_t, reverse, keep, incl, rest):
        r0 = pl.multiple_of(si * slab, slab)
        q = q_ref[pl.ds(r0, slab), :]
        v = v_ref[pl.ds(r0, slab), :]
        f = lb + (1.0 - lb) * jax.nn.sigmoid(f_ref[pl.ds(r0, slab), :])
        lf = jnp.log(f)
        kk = 1.0 - f
        b = _split_dot(incl, lf)
        c = _split_dot(rest, lf)
        q_bf = (q * jnp.exp(b)).astype(BF16)
        k_dec = (kk * jnp.exp(-b)).astype(BF16)
        k_end = (kk * jnp.exp(c)).astype(BF16)
        v_bf = v.astype(BF16)
        att = lax.dot_general(q_bf, k_dec, (((1,), (1,)), ((), ())), preferred_element_type=F32)
        att = jnp.where(keep, att, 0.0).astype(BF16)
        o_intra = jnp.dot(att, v_bf, preferred_element_type=F32)
        order = range(n_chunks - 1, -1, -1) if reverse else range(n_chunks)
        for j in order:
            lo, hi = j * chunk, (j + 1) * chunk
            o_j = o_intra[lo:hi] + lax.dot_general(
                q_bf[lo:hi], state_t.astype(BF16), (((1,), (1,)), ((), ())),
                preferred_element_type=F32)
            last = lo if reverse else hi - 1
            decay = jnp.exp(b[last:last + 1, :])
            upd = lax.dot_general(v_bf[lo:hi], k_end[lo:hi], (((0,), (0,)), ((), ())),
                                  preferred_element_type=F32)
            state_t = state_t * decay + upd
            acc_ref[pl.ds(r0 + lo, chunk), :] = o_j
        return state_t

    mask_f = masks(False)
    mask_b = masks(True)

    def body(i, states):
        st_f = slab_pass(ff_ref, accf_ref, i, states[0], False, *mask_f)
        st_b = slab_pass(fb_ref, accb_ref, n_slabs - 1 - i, states[1], True, *mask_b)
        return st_f, st_b

    zero = jnp.zeros((A_DK, A_DK), F32)
    lax.fori_loop(0, n_slabs, body, (zero, zero))

    ng = ng_ref[0]

    def finish(i, carry):
        r0 = pl.multiple_of(i * slab, slab)
        o = accf_ref[pl.ds(r0, slab), :] + accb_ref[pl.ds(r0, slab), :]
        o = o * lax.rsqrt(jnp.mean(o * o, axis=-1, keepdims=True) + RMS_EPS)
        g = g_ref[pl.ds(r0, slab), :]
        o_ref[pl.ds(r0, slab), :] = (o * ng * (g * jax.nn.sigmoid(g))).astype(o_ref.dtype)
        return carry

    lax.fori_loop(0, n_slabs, finish, 0)


def _hgrn(u, lb, norm_g, seq, batch):
    blk = lambda off: pl.BlockSpec((seq, 128), lambda b, h: (b, off + h))
    par = pl.BlockSpec((1, 1, 128), lambda b, h: (h, 0, 0))
    return pl.pallas_call(
        functools.partial(_hgrn_kernel, n_slabs=seq // HGRN_SLAB),
        out_shape=jax.ShapeDtypeStruct((batch * seq, A_WIDTH), BF16),
        grid=(batch, A_HEADS),
        in_specs=[blk(COL_QA), blk(COL_FF), blk(COL_FB), blk(COL_IA), blk(COL_GA), par, par],
        out_specs=pl.BlockSpec((seq, 128), lambda b, h: (b, h)),
        scratch_shapes=[pltpu.VMEM((seq, 128), F32), pltpu.VMEM((seq, 128), F32)],
        compiler_params=_cparams(("parallel", "parallel")),
        name="hgrn2_bidirectional",
    )(u, u, u, u, u, lb.reshape(A_HEADS, 1, A_DK), norm_g.reshape(A_HEADS, 1, 128))


def _t5_buckets(rel):
    nb = NUM_BUCKETS // 2
    ret = (rel > 0).astype(np.int32) * nb
    n = np.abs(rel)
    max_exact = nb // 2
    large = max_exact + (np.log(np.maximum(n, 1) / max_exact) / math.log(MAX_DISTANCE / max_exact)
                         * (nb - max_exact)).astype(np.int32)
    large = np.minimum(large, nb - 1)
    return ret + np.where(n < max_exact, n, large)


def _attn_kernel(q_ref, k_ref, v_ref, bias_ref, sink_ref, o_ref, kpad_ref, vpad_ref, *, seq):
    w = WINDOW
    nblk = seq // w
    zeros = jnp.zeros((w, B_HEAD_DIM), BF16)
    kpad_ref[0:w, :] = zeros
    vpad_ref[0:w, :] = zeros
    kpad_ref[w + seq:2 * w + seq, :] = zeros
    vpad_ref[w + seq:2 * w + seq, :] = zeros
    kpad_ref[w:w + seq, :] = k_ref[...].astype(BF16)
    vpad_ref[w:w + seq, :] = v_ref[...].astype(BF16)
    qi = lax.broadcasted_iota(I32, (w, 3 * w), 0)
    kj = lax.broadcasted_iota(I32, (w, 3 * w), 1)
    rel = kj - w - qi
    in_window = (rel <= w) & (rel >= -w)
    scale = B_HEAD_DIM ** -0.5

    def block(n, carry):
        r0 = pl.multiple_of(n * w, w)
        kabs = kj + (n - 1) * w
        mask = in_window & (kabs >= 0) & (kabs < seq)
        kb = kpad_ref[pl.ds(r0, 3 * w), :]
        vb = vpad_ref[pl.ds(r0, 3 * w), :]
        for g in range(B_GROUP):
            q = q_ref[pl.ds(r0, w), g * B_HEAD_DIM:(g + 1) * B_HEAD_DIM].astype(BF16)
            s = lax.dot_general(q, kb, (((1,), (1,)), ((), ())), preferred_element_type=F32)
            s = s * scale + bias_ref[g]
            s = jnp.where(mask, s, NEG_INF)
            sk = sink_ref[g][:, 0:1]
            m = jnp.maximum(jnp.max(s, axis=-1, keepdims=True), sk)
            p = jnp.exp(s - m)
            den = jnp.sum(p, axis=-1, keepdims=True) + jnp.exp(sk - m)
            o = jnp.dot(p.astype(BF16), vb, preferred_element_type=F32) / den
            o_ref[pl.ds(r0, w), g * B_HEAD_DIM:(g + 1) * B_HEAD_DIM] = o.astype(o_ref.dtype)
        return carry

    lax.fori_loop(0, nblk, block, 0)


def _window_attention(u, sink, rel_table, seq, batch):
    rel = np.arange(3 * WINDOW)[None, :] - WINDOW - np.arange(WINDOW)[:, None]
    bias = rel_table.astype(F32)[jnp.asarray(_t5_buckets(rel))]
    bias = jnp.transpose(bias, (2, 0, 1))
    sink_b = jnp.broadcast_to(sink.astype(F32)[:, None, None], (B_Q_HEADS, 1, 128))
    qw = B_GROUP * B_HEAD_DIM
    return pl.pallas_call(
        functools.partial(_attn_kernel, seq=seq),
        out_shape=jax.ShapeDtypeStruct((batch * seq, B_WIDTH), BF16),
        grid=(batch, B_KV_HEADS),
        in_specs=[pl.BlockSpec((seq, qw), lambda b, h: (b, COL_QB_512 + h)),
                  pl.BlockSpec((seq, 128), lambda b, h: (b, COL_KB + h)),
                  pl.BlockSpec((seq, 128), lambda b, h: (b, COL_VB + h)),
                  pl.BlockSpec((B_GROUP, WINDOW, 3 * WINDOW), lambda b, h: (h, 0, 0)),
                  pl.BlockSpec((B_GROUP, 1, 128), lambda b, h: (h, 0, 0))],
        out_specs=pl.BlockSpec((seq, qw), lambda b, h: (b, h)),
        scratch_shapes=[pltpu.VMEM((seq + 2 * WINDOW, B_HEAD_DIM), BF16),
                        pltpu.VMEM((seq + 2 * WINDOW, B_HEAD_DIM), BF16)],
        compiler_params=_cparams(("parallel", "parallel")),
        name="window_attention",
    )(u, u, u, bias, sink_b)


def _gelu_tanh(x):
    return 0.5 * x * (1.0 + jnp.tanh(math.sqrt(2.0 / math.pi) * (x + 0.044715 * x * x * x)))


def _rglru_kernel(xr_ref, prev_ref, next_ref, y_ref, cw_ref, cb_ref, wr_ref, br_ref, wi_ref, bi_ref,
                  sp_ref, o_ref, hsf_ref, a_ref, d_ref, hs_ref, h_ref, *, ts, n_t, batch):
    p = pl.program_id(1)
    t = pl.program_id(2)
    tb = t + p * (n_t - 1 - 2 * t)

    @pl.when(t == 0)
    def _():
        h_ref[...] = jnp.zeros_like(h_ref)

    xr = xr_ref[...]
    prev = jnp.where(tb == 0, 0.0, prev_ref[...])
    nxt = jnp.where(tb == n_t - 1, 0.0, next_ref[...])
    xfull = jnp.concatenate([prev, xr, nxt], axis=0)
    cw = cw_ref[...]
    xc = cb_ref[...].reshape(1, 1, C_BLOCK) + sum(
        xfull[j:j + ts] * cw[j:j + 1, :].reshape(1, 1, C_BLOCK) for j in range(C_CONV))
    x2 = xc.reshape(ts * batch, C_BLOCK)
    xb = x2.astype(BF16)
    r = jax.nn.sigmoid(jnp.dot(xb, wr_ref[...], preferred_element_type=F32) + br_ref[0])
    gi = jax.nn.sigmoid(jnp.dot(xb, wi_ref[...], preferred_element_type=F32) + bi_ref[0])
    log_a = (-RG_C) * r * sp_ref[0]
    a = jnp.exp(log_a)
    drive = jnp.sqrt(1.0 - a * a) * (gi * x2)
    a_ref[...] = a.reshape(ts, batch, C_BLOCK)
    d_ref[...] = drive.reshape(ts, batch, C_BLOCK)

    def step(i, h):
        idx = i + p * (ts - 1 - 2 * i)
        h = a_ref[idx] * h + d_ref[idx]
        hs_ref[idx] = h
        return h

    h_ref[...] = lax.fori_loop(0, ts, step, h_ref[...], unroll=8)
    base = pl.multiple_of(tb * ts, ts)

    @pl.when(p == 0)
    def _():
        hsf_ref[pl.ds(base, ts)] = hs_ref[...]

    @pl.when(p == 1)
    def _():
        hs = hsf_ref[pl.ds(base, ts)] + hs_ref[...]
        o_ref[...] = (hs.reshape(ts * batch, C_BLOCK) * _gelu_tanh(y_ref[...])).astype(o_ref.dtype)


def _rglru(u, conv_w, conv_b, w_r, b_r, w_i, b_i, a_param, seq, batch):
    ts = min(256, seq)
    n_t = seq // ts
    nc = C_GATE_BLOCKS
    u3 = u.reshape(seq, batch, 2 * C_WIDTH)
    sp = jax.nn.softplus(-a_param.astype(F32)).reshape(2, nc, 1, C_BLOCK)
    tblk = lambda p, t: t + p * (n_t - 1 - 2 * t)
    pinned = lambda p, t: p * tblk(p, t) + (1 - p) * (n_t - 1)
    per_dir = lambda shape: pl.BlockSpec((None, None) + shape, lambda c, p, t: (p, c, 0, 0))
    kern = functools.partial(_rglru_kernel, ts=ts, n_t=n_t, batch=batch)
    return pl.pallas_call(
        kern,
        out_shape=jax.ShapeDtypeStruct((seq * batch, C_WIDTH), BF16),
        grid=(nc, 2, n_t),
        in_specs=[
            pl.BlockSpec((ts, batch, C_BLOCK), lambda c, p, t: (tblk(p, t), 0, nc + c)),
            pl.BlockSpec((2, batch, C_BLOCK),
                         lambda c, p, t: (jnp.maximum(tblk(p, t) * (ts // 2) - 1, 0), 0, nc + c)),
            pl.BlockSpec((1, batch, C_BLOCK),
                         lambda c, p, t: (jnp.minimum((tblk(p, t) + 1) * ts, seq - 1), 0, nc + c)),
            pl.BlockSpec((ts * batch, C_BLOCK), lambda c, p, t: (pinned(p, t), c)),
            pl.BlockSpec((C_CONV, C_BLOCK), lambda c, p, t: (0, c)),
            pl.BlockSpec((1, C_BLOCK), lambda c, p, t: (0, c)),
            per_dir((C_BLOCK, C_BLOCK)), per_dir((1, C_BLOCK)),
            per_dir((C_BLOCK, C_BLOCK)), per_dir((1, C_BLOCK)),
            per_dir((1, C_BLOCK)),
        ],
        out_specs=pl.BlockSpec((ts * batch, C_BLOCK), lambda c, p, t: (pinned(p, t), c)),
        scratch_shapes=[pltpu.VMEM((seq, batch, C_BLOCK), F32),
                        pltpu.VMEM((ts, batch, C_BLOCK), F32),
                        pltpu.VMEM((ts, batch, C_BLOCK), F32),
                        pltpu.VMEM((ts, batch, C_BLOCK), F32),
                        pltpu.VMEM((batch, C_BLOCK), F32)],
        compiler_params=_cparams(("parallel", "arbitrary", "arbitrary")),
        name="rglru_bidirectional",
    )(u3, u3, u3, u, conv_w, conv_b.reshape(1, C_WIDTH),
      w_r.astype(BF16), b_r.reshape(2, nc, 1, C_BLOCK), w_i.astype(BF16), b_i.reshape(2, nc, 1, C_BLOCK), sp)


def _route_kernel(h_ref, rwt_ref, rb_ref, e1_ref, e2_ref, g1_ref, g2_ref, r1_ref, r2_ref, cnt_ref,
                  carry_ref, *, tb):
    @pl.when(pl.program_id(0) == 0)
    def _():
        carry_ref[...] = jnp.zeros_like(carry_ref)

    logits = lax.dot_general(rwt_ref[...], h_ref[...], (((1,), (1,)), ((), ())),
                             precision=lax.Precision.HIGHEST,
                             preferred_element_type=F32) + rb_ref[...]
    mx = jnp.max(logits, axis=0, keepdims=True)
    ex = jnp.exp(logits - mx)
    probs = ex / jnp.sum(ex, axis=0, keepdims=True)
    eid = lax.broadcasted_iota(I32, (N_EXPERTS, tb), 0)
    gmax = jnp.max(probs.reshape(N_GROUPS, EXPERTS_PER_GROUP, tb), axis=1)
    gid = lax.broadcasted_iota(I32, (N_GROUPS, tb), 0)
    grp = jnp.min(jnp.where(gmax == jnp.max(gmax, axis=0, keepdims=True), gid, N_GROUPS),
                  axis=0, keepdims=True)
    sel = jnp.where((eid // EXPERTS_PER_GROUP) == grp, probs, -1.0)
    m1 = jnp.max(sel, axis=0, keepdims=True)
    e1 = jnp.min(jnp.where(sel == m1, eid, N_EXPERTS), axis=0, keepdims=True)
    sel2 = jnp.where(eid == e1, -1.0, sel)
    m2 = jnp.max(sel2, axis=0, keepdims=True)
    e2 = jnp.min(jnp.where(sel2 == m2, eid, N_EXPERTS), axis=0, keepdims=True)
    tot = m1 + m2
    e1_ref[...] = e1
    e2_ref[...] = e2
    g1_ref[...] = m1 / tot
    g2_ref[...] = m2 / tot
    oh1 = eid == e1
    oh2 = eid == e2
    oh = jnp.where(oh1 | oh2, 1.0, 0.0)
    before = jnp.where(lax.broadcasted_iota(I32, (tb, tb), 0) < lax.broadcasted_iota(I32, (tb, tb), 1),
                       1.0, 0.0).astype(BF16)
    base = carry_ref[...] + jnp.dot(oh.astype(BF16), before, preferred_element_type=F32)
    r1_ref[...] = jnp.sum(jnp.where(oh1, base, 0.0), axis=0, keepdims=True).astype(I32)
    r2_ref[...] = jnp.sum(jnp.where(oh2, base, 0.0), axis=0, keepdims=True).astype(I32)
    carry_ref[...] = carry_ref[...] + jnp.sum(oh, axis=1, keepdims=True)
    cnt_ref[...] = jnp.broadcast_to(carry_ref[...], cnt_ref.shape).astype(I32)


def _route(h, router_w, router_b, tb=512):
    t, d = h.shape
    tb = min(tb, t)
    lane = lambda dt: jax.ShapeDtypeStruct((1, t), dt)
    lspec = pl.BlockSpec((1, tb), lambda i: (0, i))
    outs = pl.pallas_call(
        functools.partial(_route_kernel, tb=tb),
        out_shape=(lane(I32), lane(I32), lane(F32), lane(F32), lane(I32), lane(I32),
                   jax.ShapeDtypeStruct((N_EXPERTS, 128), I32)),
        grid=(t // tb,),
        in_specs=[pl.BlockSpec((tb, d), lambda i: (i, 0)),
                  pl.BlockSpec((N_EXPERTS, d), lambda i: (0, 0)),
                  pl.BlockSpec((N_EXPERTS, 1), lambda i: (0, 0))],
        out_specs=(lspec, lspec, lspec, lspec, lspec, lspec,
                   pl.BlockSpec((N_EXPERTS, 128), lambda i: (0, 0))),
        scratch_shapes=[pltpu.VMEM((N_EXPERTS, 1), F32)],
        compiler_params=_cparams(("arbitrary",)),
        name="moe_route",
    )(h, router_w.T, router_b.reshape(N_EXPERTS, 1).astype(F32))
    e1, e2, g1, g2, r1, r2, cnt = outs
    return e1[0], e2[0], g1[0], g2[0], r1[0], r2[0], cnt[:, 0]


def _row_copy(src_ref, src_row, dst_ref, dst_row, sem):
    return pltpu.make_async_copy(src_ref.at[pl.ds(src_row, 1)], dst_ref.at[pl.ds(dst_row, 1)], sem)


def _dispatch_kernel(d1_ref, d2_ref, h_ref, xs_in_ref, xs_ref, sems, *, tb):
    del xs_in_ref
    t0 = pl.program_id(0) * tb

    def start(r, carry):
        _row_copy(h_ref, r, xs_ref, d1_ref[t0 + r], sems.at[0]).start()
        _row_copy(h_ref, r, xs_ref, d2_ref[t0 + r], sems.at[1]).start()
        return carry

    def wait(r, carry):
        _row_copy(h_ref, r, xs_ref, d1_ref[t0 + r], sems.at[0]).wait()
        _row_copy(h_ref, r, xs_ref, d2_ref[t0 + r], sems.at[1]).wait()
        return carry

    lax.fori_loop(0, tb, start, 0)
    lax.fori_loop(0, tb, wait, 0)


def _dispatch(h, dest1, dest2, n_rows, tb=128):
    t, d = h.shape
    tb = min(tb, t)
    xs0 = jnp.zeros((n_rows, d), F32)
    return pl.pallas_call(
        functools.partial(_dispatch_kernel, tb=tb),
        out_shape=jax.ShapeDtypeStruct((n_rows, d), F32),
        grid_spec=pltpu.PrefetchScalarGridSpec(
            num_scalar_prefetch=2,
            grid=(t // tb,),
            in_specs=[pl.BlockSpec((tb, d), lambda i, d1, d2: (i, 0)),
                      pl.BlockSpec(memory_space=pl.ANY)],
            out_specs=pl.BlockSpec(memory_space=pl.ANY),
            scratch_shapes=[pltpu.SemaphoreType.DMA((2,))],
        ),
        input_output_aliases={3: 0},
        compiler_params=_cparams(("arbitrary",)),
        name="moe_dispatch",
    )(dest1, dest2, h, xs0)


def _expert_kernel(be_ref, nu_ref, x_ref, wg_ref, wu_ref, wd_ref, o_ref):
    b = pl.program_id(0)

    @pl.when(b < nu_ref[0])
    def _():
        x = x_ref[...].astype(BF16)
        hg = jnp.dot(x, wg_ref[...], preferred_element_type=F32)
        hu = jnp.dot(x, wu_ref[...], preferred_element_type=F32)
        hid = (hg * jax.nn.sigmoid(hg) * hu).astype(BF16)
        o_ref[...] = jnp.dot(hid, wd_ref[...], preferred_element_type=F32)

    @pl.when(b >= nu_ref[0])
    def _():
        o_ref[...] = jnp.zeros_like(o_ref)


def _experts(xs, block_e, n_used, w_gate, w_up, w_down):
    p, d = xs.shape
    nblk = p // MOE_BLOCK
    de = w_gate.shape[-1]
    xrow = lambda b, be, nu: (jnp.minimum(b, jnp.maximum(nu[0] - 1, 0)), 0)
    return pl.pallas_call(
        _expert_kernel,
        out_shape=jax.ShapeDtypeStruct((p, d), F32),
        grid_spec=pltpu.PrefetchScalarGridSpec(
            num_scalar_prefetch=2,
            grid=(nblk,),
            in_specs=[pl.BlockSpec((MOE_BLOCK, d), xrow),
                      pl.BlockSpec((None, d, de), lambda b, be, nu: (be[b], 0, 0)),
                      pl.BlockSpec((None, d, de), lambda b, be, nu: (be[b], 0, 0)),
                      pl.BlockSpec((None, de, d), lambda b, be, nu: (be[b], 0, 0))],
            out_specs=pl.BlockSpec((MOE_BLOCK, d), lambda b, be, nu: (b, 0)),
        ),
        compiler_params=_cparams(("arbitrary",)),
        name="moe_experts",
    )(block_e, n_used, xs, w_gate, w_up, w_down)


def _combine_ln_kernel(d1_ref, d2_ref, yb_ref, h_ref, g1_ref, g2_ref, lg_ref, lb_ref, o_ref, obf_ref,
                       y1_ref, y2_ref, sems, *, tb):
    t0 = pl.program_id(0) * tb

    def start(r, carry):
        _row_copy(yb_ref, d1_ref[t0 + r], y1_ref, r, sems.at[0]).start()
        _row_copy(yb_ref, d2_ref[t0 + r], y2_ref, r, sems.at[1]).start()
        return carry

    def wait(r, carry):
        _row_copy(yb_ref, d1_ref[t0 + r], y1_ref, r, sems.at[0]).wait()
        _row_copy(yb_ref, d2_ref[t0 + r], y2_ref, r, sems.at[1]).wait()
        return carry

    lax.fori_loop(0, tb, start, 0)
    lax.fori_loop(0, tb, wait, 0)
    y = y1_ref[...] * g1_ref[...] + y2_ref[...] * g2_ref[...]
    out = _layer_norm_rows(ALPHA * h_ref[...] + y, lg_ref[...], lb_ref[...])
    o_ref[...] = out
    obf_ref[...] = out.astype(BF16)


def _combine_ln(yb, h, dest1, dest2, g1, g2, ln_g, ln_b, tb=128):
    t, d = h.shape
    tb = min(tb, t)
    row = pl.BlockSpec((tb, d), lambda i, d1, d2: (i, 0))
    colv = pl.BlockSpec((tb, 1), lambda i, d1, d2: (i, 0))
    vec = pl.BlockSpec((1, d), lambda i, d1, d2: (0, 0))
    return pl.pallas_call(
        functools.partial(_combine_ln_kernel, tb=tb),
        out_shape=(jax.ShapeDtypeStruct((t, d), F32), jax.ShapeDtypeStruct((t, d), BF16)),
        grid_spec=pltpu.PrefetchScalarGridSpec(
            num_scalar_prefetch=2,
            grid=(t // tb,),
            in_specs=[pl.BlockSpec(memory_space=pl.ANY), row, colv, colv, vec, vec],
            out_specs=(row, row),
            scratch_shapes=[pltpu.VMEM((tb, d), F32), pltpu.VMEM((tb, d), F32),
                            pltpu.SemaphoreType.DMA((2,))],
        ),
        compiler_params=_cparams(("arbitrary",)),
        name="moe_combine_layernorm",
    )(dest1, dest2, yb, h, g1.reshape(t, 1), g2.reshape(t, 1), ln_g.reshape(1, d), ln_b.reshape(1, d))


def _moe_ln(h, router_w, router_b, w_gate, w_up, w_down, ln_g, ln_b):
    t, d = h.shape
    e1, e2, g1, g2, r1, r2, counts = _route(h, router_w, router_b)
    padded = (counts + MOE_BLOCK - 1) // MOE_BLOCK * MOE_BLOCK
    pends = jnp.cumsum(padded)
    pstarts = pends - padded
    dest1 = pstarts[e1] + r1
    dest2 = pstarts[e2] + r2
    nblk = -(-(t * TOP_K) // MOE_BLOCK) + N_EXPERTS
    n_used = (pends[-1] // MOE_BLOCK).astype(I32)
    blk_row = jnp.minimum(jnp.arange(nblk, dtype=I32), jnp.maximum(n_used - 1, 0)) * MOE_BLOCK
    block_e = jnp.minimum(jnp.sum(pends[None, :] <= blk_row[:, None], axis=1), N_EXPERTS - 1).astype(I32)
    xs = _dispatch(h, dest1, dest2, nblk * MOE_BLOCK)
    yb = _experts(xs, block_e, n_used.reshape(1), w_gate, w_up, w_down)
    return _combine_ln(yb, h, dest1, dest2, g1, g2, ln_g, ln_b)


def _swap_token_order(a, outer, inner):
    return jnp.transpose(a.reshape(outer, inner, a.shape[-1]), (1, 0, 2)).reshape(outer * inner, a.shape[-1])


def kernel(x, ab_w_in, ab_w_out, hgrn_lower_bounds, hgrn_norm_g, attn_sink, rel_bias_table, c_w_in, c_conv_w, c_conv_b, c_w_rgate, c_b_rgate, c_w_igate, c_b_igate, c_a_param, c_w_out, ln_g, ln_b, router_w, router_b, moe_w_gate, moe_w_up, moe_w_down):
    batch, seq, d = x.shape
    t = batch * seq
    lb_all = jnp.cumsum(jax.nn.softmax(hgrn_lower_bounds.astype(F32), axis=0), axis=0)
    h = x.reshape(t, d)
    h_bf = h.astype(BF16)
    batch_major = True
    for l in range(DEPTH):
        j = l // 2
        if (l % 2 == 0) != batch_major:
            outer, inner = (batch, seq) if batch_major else (seq, batch)
            h, h_bf = _swap_token_order(h, outer, inner), _swap_token_order(h_bf, outer, inner)
            batch_major = not batch_major
        if l % 2 == 0:
            u = _matmul(h_bf, ab_w_in[j].astype(BF16), F32, 1024, 1024)
            o_a = _hgrn(u, lb_all[l], hgrn_norm_g[j], seq, batch)
            o_b = _window_attention(u, attn_sink[j], rel_bias_table, seq, batch)
            mix = _matmul2(o_a, o_b, ab_w_out[j].astype(BF16), F32, 1024, 1024)
        else:
            u = _matmul(h_bf, c_w_in[j].astype(BF16), F32, 1024, 1024)
            z = _rglru(u, c_conv_w[j], c_conv_b[j], c_w_rgate[j], c_b_rgate[j], c_w_igate[j], c_b_igate[j],
                       c_a_param[j], seq, batch)
            mix = _matmul(z, c_w_out[j].astype(BF16), F32, 1024, 1024)
        h, _ = _res_ln(h, mix, ln_g[l, 0], ln_b[l, 0])
        h, h_bf = _moe_ln(h, router_w, router_b, moe_w_gate[l].astype(BF16), moe_w_up[l].astype(BF16),
                          moe_w_down[l].astype(BF16), ln_g[l, 1], ln_b[l, 1])
    if not batch_major:
        h = _swap_token_order(h, seq, batch)
    return h.reshape(batch, seq, d)
```

```python
import functools
import math

import numpy as np
import jax
import jax.numpy as jnp
from jax import lax
from jax.experimental import pallas as pl
from jax.experimental.pallas import tpu as pltpu

F32 = jnp.float32
BF16 = jnp.bfloat16
I32 = jnp.int32

D_MODEL = 4096
DEPTH = 2
A_HEADS = 16
A_DK = 128
A_WIDTH = 2048
HGRN_CHUNK = 32
HGRN_SLAB = 256
B_Q_HEADS = 16
B_KV_HEADS = 4
B_GROUP = B_Q_HEADS // B_KV_HEADS
B_HEAD_DIM = 128
B_WIDTH = 2048
WINDOW = 128
NUM_BUCKETS = 32
MAX_DISTANCE = 128
NEG_INF = -1e30
AB_IN = 13312
COL_QA, COL_FF, COL_FB, COL_IA, COL_GA = 0, 16, 32, 48, 64
COL_QB_512, COL_KB, COL_VB = 20, 96, 100
C_WIDTH = 4096
C_BLOCK = 256
C_GATE_BLOCKS = 16
C_CONV = 4
RG_C = 8.0
N_EXPERTS = 32
N_GROUPS = 4
EXPERTS_PER_GROUP = 8
TOP_K = 2
D_EXPERT = 768
MOE_BLOCK = 128
ALPHA = (2 * DEPTH) ** 0.25
LN_EPS = 1e-5
RMS_EPS = 1e-6

VMEM_LIMIT = 56 * 1024 * 1024


def _cparams(sem, vmem=VMEM_LIMIT):
    return pltpu.CompilerParams(dimension_semantics=sem, vmem_limit_bytes=vmem)


def _mm_kernel(x_ref, w_ref, o_ref):
    o_ref[...] = jnp.dot(x_ref[...], w_ref[...], preferred_element_type=F32).astype(o_ref.dtype)


def _matmul(x, w, out_dtype, tm, tn):
    m, k = x.shape
    n = w.shape[1]
    tm, tn = min(tm, m), min(tn, n)
    return pl.pallas_call(
        _mm_kernel,
        out_shape=jax.ShapeDtypeStruct((m, n), out_dtype),
        grid=(m // tm, n // tn),
        in_specs=[pl.BlockSpec((tm, k), lambda i, j: (i, 0)),
                  pl.BlockSpec((k, tn), lambda i, j: (0, j))],
        out_specs=pl.BlockSpec((tm, tn), lambda i, j: (i, j)),
        compiler_params=_cparams(("parallel", "arbitrary")),
        name="dense_matmul",
    )(x, w)


def _mm2_kernel(xa_ref, xb_ref, w_ref, o_ref, *, ka):
    acc = jnp.dot(xa_ref[...], w_ref[0:ka, :], preferred_element_type=F32)
    acc = acc + jnp.dot(xb_ref[...], w_ref[ka:, :], preferred_element_type=F32)
    o_ref[...] = acc.astype(o_ref.dtype)


def _matmul2(xa, xb, w, out_dtype, tm, tn):
    m, ka = xa.shape
    kb = xb.shape[1]
    n = w.shape[1]
    tm, tn = min(tm, m), min(tn, n)
    return pl.pallas_call(
        functools.partial(_mm2_kernel, ka=ka),
        out_shape=jax.ShapeDtypeStruct((m, n), out_dtype),
        grid=(m // tm, n // tn),
        in_specs=[pl.BlockSpec((tm, ka), lambda i, j: (i, 0)),
                  pl.BlockSpec((tm, kb), lambda i, j: (i, 0)),
                  pl.BlockSpec((ka + kb, tn), lambda i, j: (0, j))],
        out_specs=pl.BlockSpec((tm, tn), lambda i, j: (i, j)),
        compiler_params=_cparams(("parallel", "arbitrary")),
        name="dense_matmul_2in",
    )(xa, xb, w)


def _layer_norm_rows(z, g, b):
    mu = jnp.mean(z, axis=-1, keepdims=True)
    zc = z - mu
    var = jnp.mean(zc * zc, axis=-1, keepdims=True)
    return zc * lax.rsqrt(var + LN_EPS) * g + b


def _res_ln_kernel(h_ref, mix_ref, g_ref, b_ref, o_ref, obf_ref):
    z = ALPHA * h_ref[...] + mix_ref[...]
    out = _layer_norm_rows(z, g_ref[...], b_ref[...])
    o_ref[...] = out
    obf_ref[...] = out.astype(BF16)


def _res_ln(h, mix, g, b, tb=256):
    t, d = h.shape
    tb = min(tb, t)
    row = pl.BlockSpec((tb, d), lambda i: (i, 0))
    vec = pl.BlockSpec((1, d), lambda i: (0, 0))
    return pl.pallas_call(
        _res_ln_kernel,
        out_shape=(jax.ShapeDtypeStruct((t, d), F32), jax.ShapeDtypeStruct((t, d), BF16)),
        grid=(t // tb,),
        in_specs=[row, row, vec, vec],
        out_specs=(row, row),
        compiler_params=_cparams(("parallel",)),
        name="residual_layernorm",
    )(h, mix, g.reshape(1, d), b.reshape(1, d))


def _split_dot(mat_bf, x):
    hi = x.astype(BF16)
    lo = (x - hi.astype(F32)).astype(BF16)
    return (jnp.dot(mat_bf, hi, preferred_element_type=F32)
            + jnp.dot(mat_bf, lo, preferred_element_type=F32))


def _hgrn_kernel(q_ref, ff_ref, fb_ref, v_ref, g_ref, lb_ref, ng_ref, o_ref, accf_ref, accb_ref, *, n_slabs):
    slab, chunk = HGRN_SLAB, HGRN_CHUNK
    n_chunks = slab // chunk
    row = lax.broadcasted_iota(I32, (slab, slab), 0)
    col = lax.broadcasted_iota(I32, (slab, slab), 1)
    same_chunk = (row // chunk) == (col // chunk)
    lb = lb_ref[0]

    def masks(reverse):
        keep = same_chunk & ((col >= row) if reverse else (col <= row))
        keep_strict = same_chunk & ((col < row) if reverse else (col > row))
        incl = jnp.where(keep, 1.0, 0.0).astype(BF16)
        rest = jnp.where(keep_strict, 1.0, 0.0).astype(BF16)
        return keep, incl, rest

    def slab_pass(f_ref, acc_ref, si, state_t, reverse, keep, incl, rest):
        r0 = pl.multiple_of(si * slab, slab)
        q = q_ref[pl.ds(r0, slab), :]
        v = v_ref[pl.ds(r0, slab), :]
        f = lb + (1.0 - lb) * jax.nn.sigmoid(f_ref[pl.ds(r0, slab), :])
        lf = jnp.log(f)
        kk = 1.0 - f
        b = _split_dot(incl, lf)
        c = _split_dot(rest, lf)
        q_bf = (q * jnp.exp(b)).astype(BF16)
        k_dec = (kk * jnp.exp(-b)).astype(BF16)
        k_end = (kk * jnp.exp(c)).astype(BF16)
        v_bf = v.astype(BF16)
        att = lax.dot_general(q_bf, k_dec, (((1,), (1,)), ((), ())), preferred_element_type=F32)
        att = jnp.where(keep, att, 0.0).astype(BF16)
        o_intra = jnp.dot(att, v_bf, preferred_element_type=F32)
        spans = [(j * chunk, (j + 1) * chunk) for j in range(n_chunks)]
        upd = [lax.dot_general(v_bf[lo:hi], k_end[lo:hi], (((0,), (0,)), ((), ())),
                               preferred_element_type=F32) for lo, hi in spans]
        order = range(n_chunks - 1, -1, -1) if reverse else range(n_chunks)
        state_in = [None] * n_chunks
        for j in order:
            lo, hi = spans[j]
            last = lo if reverse else hi - 1
            state_in[j] = state_t.astype(BF16)
            state_t = state_t * jnp.exp(b[last:last + 1, :]) + upd[j]
        for j in range(n_chunks):
            lo, hi = spans[j]
            o_inter = lax.dot_general(q_bf[lo:hi], state_in[j], (((1,), (1,)), ((), ())),
                                      preferred_element_type=F32)
            acc_ref[pl.ds(r0 + lo, chunk), :] = o_intra[lo:hi] + o_inter
        return state_t

    mask_f = masks(False)
    mask_b = masks(True)

    def body(i, states):
        st_f = slab_pass(ff_ref, accf_ref, i, states[0], False, *mask_f)
        st_b = slab_pass(fb_ref, accb_ref, n_slabs - 1 - i, states[1], True, *mask_b)
        return st_f, st_b

    zero = jnp.zeros((A_DK, A_DK), F32)
    lax.fori_loop(0, n_slabs, body, (zero, zero))

    ng = ng_ref[0]

    def finish(i, carry):
        r0 = pl.multiple_of(i * slab, slab)
        o = accf_ref[pl.ds(r0, slab), :] + accb_ref[pl.ds(r0, slab), :]
        o = o * lax.rsqrt(jnp.mean(o * o, axis=-1, keepdims=True) + RMS_EPS)
        g = g_ref[pl.ds(r0, slab), :]
        o_ref[pl.ds(r0, slab), :] = (o * ng * (g * jax.nn.sigmoid(g))).astype(o_ref.dtype)
        return carry

    lax.fori_loop(0, n_slabs, finish, 0)


def _hgrn(u, lb, norm_g, seq, batch):
    blk = lambda off: pl.BlockSpec((seq, 128), lambda b, h: (b, off + h))
    par = pl.BlockSpec((1, 1, 128), lambda b, h: (h, 0, 0))
    return pl.pallas_call(
        functools.partial(_hgrn_kernel, n_slabs=seq // HGRN_SLAB),
        out_shape=jax.ShapeDtypeStruct((batch * seq, A_WIDTH), BF16),
        grid=(batch, A_HEADS),
        in_specs=[blk(COL_QA), blk(COL_FF), blk(COL_FB), blk(COL_IA), blk(COL_GA), par, par],
        out_specs=pl.BlockSpec((seq, 128), lambda b, h: (b, h)),
        scratch_shapes=[pltpu.VMEM((seq, 128), F32), pltpu.VMEM((seq, 128), F32)],
        compiler_params=_cparams(("parallel", "parallel")),
        name="hgrn2_bidirectional",
    )(u, u, u, u, u, lb.reshape(A_HEADS, 1, A_DK), norm_g.reshape(A_HEADS, 1, 128))


def _t5_buckets(rel):
    nb = NUM_BUCKETS // 2
    ret = (rel > 0).astype(np.int32) * nb
    n = np.abs(rel)
    max_exact = nb // 2
    large = max_exact + (np.log(np.maximum(n, 1) / max_exact) / math.log(MAX_DISTANCE / max_exact)
                         * (nb - max_exact)).astype(np.int32)
    large = np.minimum(large, nb - 1)
    return ret + np.where(n < max_exact, n, large)


def _attn_kernel(q_ref, k_ref, v_ref, bias_ref, sink_ref, o_ref, kpad_ref, vpad_ref, *, seq):
    w = WINDOW
    nblk = seq // w
    zeros = jnp.zeros((w, B_HEAD_DIM), BF16)
    kpad_ref[0:w, :] = zeros
    vpad_ref[0:w, :] = zeros
    kpad_ref[w + seq:2 * w + seq, :] = zeros
    vpad_ref[w + seq:2 * w + seq, :] = zeros
    kpad_ref[w:w + seq, :] = k_ref[...].astype(BF16)
    vpad_ref[w:w + seq, :] = v_ref[...].astype(BF16)
    qi = lax.broadcasted_iota(I32, (w, 3 * w), 0)
    kj = lax.broadcasted_iota(I32, (w, 3 * w), 1)
    rel = kj - w - qi
    in_window = (rel <= w) & (rel >= -w)
    scale = B_HEAD_DIM ** -0.5

    def block(n, carry):
        r0 = pl.multiple_of(n * w, w)
        kabs = kj + (n - 1) * w
        mask = in_window & (kabs >= 0) & (kabs < seq)
        kb = kpad_ref[pl.ds(r0, 3 * w), :]
        vb = vpad_ref[pl.ds(r0, 3 * w), :]
        for g in range(B_GROUP):
            q = q_ref[pl.ds(r0, w), g * B_HEAD_DIM:(g + 1) * B_HEAD_DIM].astype(BF16)
            s = lax.dot_general(q, kb, (((1,), (1,)), ((), ())), preferred_element_type=F32)
            s = s * scale + bias_ref[g]
            s = jnp.where(mask, s, NEG_INF)
            sk = sink_ref[g][:, 0:1]
            m = jnp.maximum(jnp.max(s, axis=-1, keepdims=True), sk)
            p = jnp.exp(s - m)
            den = jnp.sum(p, axis=-1, keepdims=True) + jnp.exp(sk - m)
            o = jnp.dot(p.astype(BF16), vb, preferred_element_type=F32) / den
            o_ref[pl.ds(r0, w), g * B_HEAD_DIM:(g + 1) * B_HEAD_DIM] = o.astype(o_ref.dtype)
        return carry

    lax.fori_loop(0, nblk, block, 0)


def _window_attention(u, sink, rel_table, seq, batch):
    rel = np.arange(-(2 * WINDOW - 1), 2 * WINDOW)
    per_rel = rel_table.astype(F32)[jnp.asarray(_t5_buckets(rel))].T
    bias = jnp.stack([per_rel[:, WINDOW - 1 - q:4 * WINDOW - 1 - q] for q in range(WINDOW)], axis=1)
    sink_b = jnp.broadcast_to(sink.astype(F32)[:, None, None], (B_Q_HEADS, 1, 128))
    qw = B_GROUP * B_HEAD_DIM
    return pl.pallas_call(
        functools.partial(_attn_kernel, seq=seq),
        out_shape=jax.ShapeDtypeStruct((batch * seq, B_WIDTH), BF16),
        grid=(batch, B_KV_HEADS),
        in_specs=[pl.BlockSpec((seq, qw), lambda b, h: (b, COL_QB_512 + h)),
                  pl.BlockSpec((seq, 128), lambda b, h: (b, COL_KB + h)),
                  pl.BlockSpec((seq, 128), lambda b, h: (b, COL_VB + h)),
                  pl.BlockSpec((B_GROUP, WINDOW, 3 * WINDOW), lambda b, h: (h, 0, 0)),
                  pl.BlockSpec((B_GROUP, 1, 128), lambda b, h: (h, 0, 0))],
        out_specs=pl.BlockSpec((seq, qw), lambda b, h: (b, h)),
        scratch_shapes=[pltpu.VMEM((seq + 2 * WINDOW, B_HEAD_DIM), BF16),
                        pltpu.VMEM((seq + 2 * WINDOW, B_HEAD_DIM), BF16)],
        compiler_params=_cparams(("parallel", "parallel")),
        name="window_attention",
    )(u, u, u, bias, sink_b)


def _gelu_tanh(x):
    return 0.5 * x * (1.0 + jnp.tanh(math.sqrt(2.0 / math.pi) * (x + 0.044715 * x * x * x)))


def _rglru_kernel(xr_ref, prev_ref, next_ref, y_ref, cw_ref, cb_ref, wr_ref, br_ref, wi_ref, bi_ref,
                  sp_ref, o_ref, hsf_ref, a_ref, d_ref, hs_ref, h_ref, *, ts, n_t, batch):
    p = pl.program_id(1)
    t = pl.program_id(2)
    tb = t + p * (n_t - 1 - 2 * t)

    @pl.when(t == 0)
    def _():
        h_ref[...] = jnp.zeros_like(h_ref)

    xr = xr_ref[...]
    prev = jnp.where(tb == 0, 0.0, prev_ref[...])
    nxt = jnp.where(tb == n_t - 1, 0.0, next_ref[...])
    xfull = jnp.concatenate([prev, xr, nxt], axis=0)
    cw = cw_ref[...]
    xc = cb_ref[...].reshape(1, 1, C_BLOCK) + sum(
        xfull[j:j + ts] * cw[j:j + 1, :].reshape(1, 1, C_BLOCK) for j in range(C_CONV))
    x2 = xc.reshape(ts * batch, C_BLOCK)
    xb = x2.astype(BF16)
    r = jax.nn.sigmoid(jnp.dot(xb, wr_ref[...], preferred_element_type=F32) + br_ref[0])
    gi = jax.nn.sigmoid(jnp.dot(xb, wi_ref[...], preferred_element_type=F32) + bi_ref[0])
    log_a = (-RG_C) * r * sp_ref[0]
    a = jnp.exp(log_a)
    drive = jnp.sqrt(1.0 - a * a) * (gi * x2)
    a_ref[...] = a.reshape(ts, batch, C_BLOCK)
    d_ref[...] = drive.reshape(ts, batch, C_BLOCK)

    def step(i, h):
        idx = i + p * (ts - 1 - 2 * i)
        h = a_ref[idx] * h + d_ref[idx]
        hs_ref[idx] = h
        return h

    h_ref[...] = lax.fori_loop(0, ts, step, h_ref[...], unroll=8)
    base = pl.multiple_of(tb * ts, ts)

    @pl.when(p == 0)
    def _():
        hsf_ref[pl.ds(base, ts)] = hs_ref[...]

    @pl.when(p == 1)
    def _():
        hs = hsf_ref[pl.ds(base, ts)] + hs_ref[...]
        o_ref[...] = (hs.reshape(ts * batch, C_BLOCK) * _gelu_tanh(y_ref[...])).astype(o_ref.dtype)


def _rglru(u, conv_w, conv_b, w_r, b_r, w_i, b_i, a_param, seq, batch):
    ts = min(256, seq)
    n_t = seq // ts
    nc = C_GATE_BLOCKS
    u3 = u.reshape(seq, batch, 2 * C_WIDTH)
    sp = jax.nn.softplus(-a_param.astype(F32)).reshape(2, nc, 1, C_BLOCK)
    tblk = lambda p, t: t + p * (n_t - 1 - 2 * t)
    pinned = lambda p, t: p * tblk(p, t) + (1 - p) * (n_t - 1)
    per_dir = lambda shape: pl.BlockSpec((None, None) + shape, lambda c, p, t: (p, c, 0, 0))
    kern = functools.partial(_rglru_kernel, ts=ts, n_t=n_t, batch=batch)
    return pl.pallas_call(
        kern,
        out_shape=jax.ShapeDtypeStruct((seq * batch, C_WIDTH), BF16),
        grid=(nc, 2, n_t),
        in_specs=[
            pl.BlockSpec((ts, batch, C_BLOCK), lambda c, p, t: (tblk(p, t), 0, nc + c)),
            pl.BlockSpec((2, batch, C_BLOCK),
                         lambda c, p, t: (jnp.maximum(tblk(p, t) * (ts // 2) - 1, 0), 0, nc + c)),
            pl.BlockSpec((1, batch, C_BLOCK),
                         lambda c, p, t: (jnp.minimum((tblk(p, t) + 1) * ts, seq - 1), 0, nc + c)),
            pl.BlockSpec((ts * batch, C_BLOCK), lambda c, p, t: (pinned(p, t), c)),
            pl.BlockSpec((C_CONV, C_BLOCK), lambda c, p, t: (0, c)),
            pl.BlockSpec((1, C_BLOCK), lambda c, p, t: (0, c)),
            per_dir((C_BLOCK, C_BLOCK)), per_dir((1, C_BLOCK)),
            per_dir((C_BLOCK, C_BLOCK)), per_dir((1, C_BLOCK)),
            per_dir((1, C_BLOCK)),
        ],
        out_specs=pl.BlockSpec((ts * batch, C_BLOCK), lambda c, p, t: (pinned(p, t), c)),
        scratch_shapes=[pltpu.VMEM((seq, batch, C_BLOCK), F32),
                        pltpu.VMEM((ts, batch, C_BLOCK), F32),
                        pltpu.VMEM((ts, batch, C_BLOCK), F32),
                        pltpu.VMEM((ts, batch, C_BLOCK), F32),
                        pltpu.VMEM((batch, C_BLOCK), F32)],
        compiler_params=_cparams(("parallel", "arbitrary", "arbitrary")),
        name="rglru_bidirectional",
    )(u3, u3, u3, u, conv_w, conv_b.reshape(1, C_WIDTH),
      w_r.astype(BF16), b_r.reshape(2, nc, 1, C_BLOCK), w_i.astype(BF16), b_i.reshape(2, nc, 1, C_BLOCK), sp)


def _route_kernel(h_ref, rwt_ref, rb_ref, e1_ref, e2_ref, g1_ref, g2_ref, r1_ref, r2_ref, cnt_ref,
                  carry_ref, *, tb):
    @pl.when(pl.program_id(0) == 0)
    def _():
        carry_ref[...] = jnp.zeros_like(carry_ref)

    logits = lax.dot_general(rwt_ref[...], h_ref[...], (((1,), (1,)), ((), ())),
                             precision=lax.Precision.HIGHEST,
                             preferred_element_type=F32) + rb_ref[...]
    mx = jnp.max(logits, axis=0, keepdims=True)
    ex = jnp.exp(logits - mx)
    probs = ex / jnp.sum(ex, axis=0, keepdims=True)
    eid = lax.broadcasted_iota(I32, (N_EXPERTS, tb), 0)
    gmax = jnp.max(probs.reshape(N_GROUPS, EXPERTS_PER_GROUP, tb), axis=1)
    gid = lax.broadcasted_iota(I32, (N_GROUPS, tb), 0)
    grp = jnp.min(jnp.where(gmax == jnp.max(gmax, axis=0, keepdims=True), gid, N_GROUPS),
                  axis=0, keepdims=True)
    sel = jnp.where((eid // EXPERTS_PER_GROUP) == grp, probs, -1.0)
    m1 = jnp.max(sel, axis=0, keepdims=True)
    e1 = jnp.min(jnp.where(sel == m1, eid, N_EXPERTS), axis=0, keepdims=True)
    sel2 = jnp.where(eid == e1, -1.0, sel)
    m2 = jnp.max(sel2, axis=0, keepdims=True)
    e2 = jnp.min(jnp.where(sel2 == m2, eid, N_EXPERTS), axis=0, keepdims=True)
    tot = m1 + m2
    e1_ref[...] = e1
    e2_ref[...] = e2
    g1_ref[...] = m1 / tot
    g2_ref[...] = m2 / tot
    oh1 = eid == e1
    oh2 = eid == e2
    oh = jnp.where(oh1 | oh2, 1.0, 0.0)
    before = jnp.where(lax.broadcasted_iota(I32, (tb, tb), 0) < lax.broadcasted_iota(I32, (tb, tb), 1),
                       1.0, 0.0).astype(BF16)
    base = carry_ref[...] + jnp.dot(oh.astype(BF16), before, preferred_element_type=F32)
    r1_ref[...] = jnp.sum(jnp.where(oh1, base, 0.0), axis=0, keepdims=True).astype(I32)
    r2_ref[...] = jnp.sum(jnp.where(oh2, base, 0.0), axis=0, keepdims=True).astype(I32)
    carry_ref[...] = carry_ref[...] + jnp.sum(oh, axis=1, keepdims=True)
    cnt_ref[...] = jnp.broadcast_to(carry_ref[...], cnt_ref.shape).astype(I32)


def _route(h, router_w, router_b, tb=512):
    t, d = h.shape
    tb = min(tb, t)
    lane = lambda dt: jax.ShapeDtypeStruct((1, t), dt)
    lspec = pl.BlockSpec((1, tb), lambda i: (0, i))
    outs = pl.pallas_call(
        functools.partial(_route_kernel, tb=tb),
        out_shape=(lane(I32), lane(I32), lane(F32), lane(F32), lane(I32), lane(I32),
                   jax.ShapeDtypeStruct((N_EXPERTS, 128), I32)),
        grid=(t // tb,),
        in_specs=[pl.BlockSpec((tb, d), lambda i: (i, 0)),
                  pl.BlockSpec((N_EXPERTS, d), lambda i: (0, 0)),
                  pl.BlockSpec((N_EXPERTS, 1), lambda i: (0, 0))],
        out_specs=(lspec, lspec, lspec, lspec, lspec, lspec,
                   pl.BlockSpec((N_EXPERTS, 128), lambda i: (0, 0))),
        scratch_shapes=[pltpu.VMEM((N_EXPERTS, 1), F32)],
        compiler_params=_cparams(("arbitrary",)),
        name="moe_route",
    )(h, router_w.T, router_b.reshape(N_EXPERTS, 1).astype(F32))
    e1, e2, g1, g2, r1, r2, cnt = outs
    return e1[0], e2[0], g1[0], g2[0], r1[0], r2[0], cnt[:, 0]


def _row_copy(src_ref, src_row, dst_ref, dst_row, sem):
    return pltpu.make_async_copy(src_ref.at[pl.ds(src_row, 1)], dst_ref.at[pl.ds(dst_row, 1)], sem)


def _dispatch_kernel(d1_ref, d2_ref, h_ref, xs_in_ref, xs_ref, sems, *, tb):
    del xs_in_ref
    t0 = pl.program_id(0) * tb

    def start(r, carry):
        _row_copy(h_ref, r, xs_ref, d1_ref[t0 + r], sems.at[0]).start()
        _row_copy(h_ref, r, xs_ref, d2_ref[t0 + r], sems.at[1]).start()
        return carry

    def wait(r, carry):
        _row_copy(h_ref, r, xs_ref, d1_ref[t0 + r], sems.at[0]).wait()
        _row_copy(h_ref, r, xs_ref, d2_ref[t0 + r], sems.at[1]).wait()
        return carry

    lax.fori_loop(0, tb, start, 0)
    lax.fori_loop(0, tb, wait, 0)


def _dispatch(h, dest1, dest2, n_rows, tb=128):
    t, d = h.shape
    tb = min(tb, t)
    xs0 = jnp.zeros((n_rows, d), F32)
    return pl.pallas_call(
        functools.partial(_dispatch_kernel, tb=tb),
        out_shape=jax.ShapeDtypeStruct((n_rows, d), F32),
        grid_spec=pltpu.PrefetchScalarGridSpec(
            num_scalar_prefetch=2,
            grid=(t // tb,),
            in_specs=[pl.BlockSpec((tb, d), lambda i, d1, d2: (i, 0)),
                      pl.BlockSpec(memory_space=pl.ANY)],
            out_specs=pl.BlockSpec(memory_space=pl.ANY),
            scratch_shapes=[pltpu.SemaphoreType.DMA((2,))],
        ),
        input_output_aliases={3: 0},
        compiler_params=_cparams(("arbitrary",)),
        name="moe_dispatch",
    )(dest1, dest2, h, xs0)


def _expert_kernel(be_ref, nu_ref, x_ref, wg_ref, wu_ref, wd_ref, o_ref):
    b = pl.program_id(0)

    @pl.when(b < nu_ref[0])
    def _():
        x = x_ref[...].astype(BF16)
        hg = jnp.dot(x, wg_ref[...], preferred_element_type=F32)
        hu = jnp.dot(x, wu_ref[...], preferred_element_type=F32)
        hid = (hg * jax.nn.sigmoid(hg) * hu).astype(BF16)
        o_ref[...] = jnp.dot(hid, wd_ref[...], preferred_element_type=F32)

    @pl.when(b >= nu_ref[0])
    def _():
        o_ref[...] = jnp.zeros_like(o_ref)


def _experts(xs, block_e, n_used, w_gate, w_up, w_down, layer):
    p, d = xs.shape
    nblk = p // MOE_BLOCK
    de = w_gate.shape[-1]
    xrow = lambda b, be, nu: (jnp.minimum(b, jnp.maximum(nu[0] - 1, 0)), 0)
    wsel = lambda b, be, nu: (layer, be[b], 0, 0)
    return pl.pallas_call(
        _expert_kernel,
        out_shape=jax.ShapeDtypeStruct((p, d), F32),
        grid_spec=pltpu.PrefetchScalarGridSpec(
            num_scalar_prefetch=2,
            grid=(nblk,),
            in_specs=[pl.BlockSpec((MOE_BLOCK, d), xrow),
                      pl.BlockSpec((None, None, d, de), wsel),
                      pl.BlockSpec((None, None, d, de), wsel),
                      pl.BlockSpec((None, None, de, d), wsel)],
            out_specs=pl.BlockSpec((MOE_BLOCK, d), lambda b, be, nu: (b, 0)),
        ),
        compiler_params=_cparams(("arbitrary",)),
        name="moe_experts",
    )(block_e, n_used, xs, w_gate, w_up, w_down)


def _combine_ln_kernel(d1_ref, d2_ref, yb_ref, h_ref, g1_ref, g2_ref, lg_ref, lb_ref, o_ref, obf_ref,
                       y1_ref, y2_ref, sems, *, tb):
    i = pl.program_id(0)
    slot = i % 2

    def copies(blk, s, r):
        t = blk * tb + r
        return (_row_copy(yb_ref, d1_ref[t], y1_ref.at[s], r, sems.at[s, 0]),
                _row_copy(yb_ref, d2_ref[t], y2_ref.at[s], r, sems.at[s, 1]))

    def gather(blk, s, wait):
        def body(r, carry):
            for c in copies(blk, s, r):
                c.wait() if wait else c.start()
            return carry
        lax.fori_loop(0, tb, body, 0)

    @pl.when(i == 0)
    def _():
        gather(0, 0, False)

    @pl.when(i + 1 < pl.num_programs(0))
    def _():
        gather(i + 1, 1 - slot, False)

    gather(i, slot, True)
    y = y1_ref[slot] * g1_ref[...] + y2_ref[slot] * g2_ref[...]
    out = _layer_norm_rows(ALPHA * h_ref[...] + y, lg_ref[...], lb_ref[...])
    o_ref[...] = out
    obf_ref[...] = out.astype(BF16)


def _combine_ln(yb, h, dest1, dest2, g1, g2, ln_g, ln_b, tb=128):
    t, d = h.shape
    tb = min(tb, t)
    row = pl.BlockSpec((tb, d), lambda i, d1, d2: (i, 0))
    colv = pl.BlockSpec((tb, 1), lambda i, d1, d2: (i, 0))
    vec = pl.BlockSpec((1, d), lambda i, d1, d2: (0, 0))
    return pl.pallas_call(
        functools.partial(_combine_ln_kernel, tb=tb),
        out_shape=(jax.ShapeDtypeStruct((t, d), F32), jax.ShapeDtypeStruct((t, d), BF16)),
        grid_spec=pltpu.PrefetchScalarGridSpec(
            num_scalar_prefetch=2,
            grid=(t // tb,),
            in_specs=[pl.BlockSpec(memory_space=pl.ANY), row, colv, colv, vec, vec],
            out_specs=(row, row),
            scratch_shapes=[pltpu.VMEM((2, tb, d), F32), pltpu.VMEM((2, tb, d), F32),
                            pltpu.SemaphoreType.DMA((2, 2))],
        ),
        compiler_params=_cparams(("arbitrary",)),
        name="moe_combine_layernorm",
    )(dest1, dest2, yb, h, g1.reshape(t, 1), g2.reshape(t, 1), ln_g.reshape(1, d), ln_b.reshape(1, d))


def _moe_ln(h, router_w, router_b, w_gate, w_up, w_down, layer, ln_g, ln_b):
    t, d = h.shape
    e1, e2, g1, g2, r1, r2, counts = _route(h, router_w, router_b)
    padded = (counts + MOE_BLOCK - 1) // MOE_BLOCK * MOE_BLOCK
    pends = jnp.cumsum(padded)
    pstarts = pends - padded
    dest1 = pstarts[e1] + r1
    dest2 = pstarts[e2] + r2
    nblk = -(-(t * TOP_K) // MOE_BLOCK) + N_EXPERTS
    n_used = (pends[-1] // MOE_BLOCK).astype(I32)
    blk_row = jnp.minimum(jnp.arange(nblk, dtype=I32), jnp.maximum(n_used - 1, 0)) * MOE_BLOCK
    block_e = jnp.minimum(jnp.sum(pends[None, :] <= blk_row[:, None], axis=1), N_EXPERTS - 1).astype(I32)
    xs = _dispatch(h, dest1, dest2, nblk * MOE_BLOCK)
    yb = _experts(xs, block_e, n_used.reshape(1), w_gate, w_up, w_down, layer)
    return _combine_ln(yb, h, dest1, dest2, g1, g2, ln_g, ln_b)


def _swap_token_order(a, outer, inner):
    return jnp.transpose(a.reshape(outer, inner, a.shape[-1]), (1, 0, 2)).reshape(outer * inner, a.shape[-1])


def kernel(x, ab_w_in, ab_w_out, hgrn_lower_bounds, hgrn_norm_g, attn_sink, rel_bias_table, c_w_in, c_conv_w, c_conv_b, c_w_rgate, c_b_rgate, c_w_igate, c_b_igate, c_a_param, c_w_out, ln_g, ln_b, router_w, router_b, moe_w_gate, moe_w_up, moe_w_down):
    batch, seq, d = x.shape
    t = batch * seq
    lb_all = jnp.cumsum(jax.nn.softmax(hgrn_lower_bounds.astype(F32), axis=0), axis=0)
    h = x.reshape(t, d)
    h_bf = h.astype(BF16)
    wg_bf, wu_bf, wd_bf = moe_w_gate.astype(BF16), moe_w_up.astype(BF16), moe_w_down.astype(BF16)
    batch_major = True
    for l in range(DEPTH):
        j = l // 2
        if (l % 2 == 0) != batch_major:
            outer, inner = (batch, seq) if batch_major else (seq, batch)
            h, h_bf = _swap_token_order(h, outer, inner), _swap_token_order(h_bf, outer, inner)
            batch_major = not batch_major
        if l % 2 == 0:
            u = _matmul(h_bf, ab_w_in[j].astype(BF16), F32, 1024, 1024)
            o_a = _hgrn(u, lb_all[l], hgrn_norm_g[j], seq, batch)
            o_b = _window_attention(u, attn_sink[j], rel_bias_table, seq, batch)
            mix = _matmul2(o_a, o_b, ab_w_out[j].astype(BF16), F32, 1024, 1024)
        else:
            u = _matmul(h_bf, c_w_in[j].astype(BF16), F32, 1024, 1024)
            z = _rglru(u, c_conv_w[j], c_conv_b[j], c_w_rgate[j], c_b_rgate[j], c_w_igate[j], c_b_igate[j],
                       c_a_param[j], seq, batch)
            mix = _matmul(z, c_w_out[j].astype(BF16), F32, 1024, 1024)
        h, _ = _res_ln(h, mix, ln_g[l, 0], ln_b[l, 0])
        h, h_bf = _moe_ln(h, router_w, router_b, wg_bf, wu_bf, wd_bf, l, ln_g[l, 1], ln_b[l, 1])
    if not batch_major:
        h = _swap_token_order(h, seq, batch)
    return h.reshape(batch, seq, d)
```

```python
import functools
import math

import numpy as np
import jax
import jax.numpy as jnp
from jax import lax
from jax.experimental import pallas as pl
from jax.experimental.pallas import tpu as pltpu

F32 = jnp.float32
BF16 = jnp.bfloat16
I32 = jnp.int32

D_MODEL = 4096
DEPTH = 2
A_HEADS = 16
A_DK = 128
A_WIDTH = 2048
HGRN_CHUNK = 32
HGRN_SLAB = 256
B_Q_HEADS = 16
B_KV_HEADS = 4
B_GROUP = B_Q_HEADS // B_KV_HEADS
B_HEAD_DIM = 128
B_WIDTH = 2048
WINDOW = 128
NUM_BUCKETS = 32
MAX_DISTANCE = 128
NEG_INF = -1e30
AB_IN = 13312
COL_QA, COL_FF, COL_FB, COL_IA, COL_GA = 0, 16, 32, 48, 64
COL_QB_512, COL_KB, COL_VB = 20, 96, 100
C_WIDTH = 4096
C_BLOCK = 256
C_GATE_BLOCKS = 16
C_CONV = 4
RG_C = 8.0
N_EXPERTS = 32
N_GROUPS = 4
EXPERTS_PER_GROUP = 8
TOP_K = 2
D_EXPERT = 768
MOE_BLOCK = 128
ALPHA = (2 * DEPTH) ** 0.25
LN_EPS = 1e-5
RMS_EPS = 1e-6

VMEM_LIMIT = 56 * 1024 * 1024


def _cparams(sem, vmem=VMEM_LIMIT):
    return pltpu.CompilerParams(dimension_semantics=sem, vmem_limit_bytes=vmem)


def _mm_kernel(x_ref, w_ref, o_ref):
    o_ref[...] = jnp.dot(x_ref[...], w_ref[...], preferred_element_type=F32).astype(o_ref.dtype)


def _matmul(x, w, out_dtype, tm, tn):
    m, k = x.shape
    n = w.shape[1]
    tm, tn = min(tm, m), min(tn, n)
    return pl.pallas_call(
        _mm_kernel,
        out_shape=jax.ShapeDtypeStruct((m, n), out_dtype),
        grid=(m // tm, n // tn),
        in_specs=[pl.BlockSpec((tm, k), lambda i, j: (i, 0)),
                  pl.BlockSpec((k, tn), lambda i, j: (0, j))],
        out_specs=pl.BlockSpec((tm, tn), lambda i, j: (i, j)),
        compiler_params=_cparams(("parallel", "arbitrary")),
        name="dense_matmul",
    )(x, w)


def _mm2_kernel(xa_ref, xb_ref, w_ref, o_ref, *, ka):
    acc = jnp.dot(xa_ref[...], w_ref[0:ka, :], preferred_element_type=F32)
    acc = acc + jnp.dot(xb_ref[...], w_ref[ka:, :], preferred_element_type=F32)
    o_ref[...] = acc.astype(o_ref.dtype)


def _matmul2(xa, xb, w, out_dtype, tm, tn):
    m, ka = xa.shape
    kb = xb.shape[1]
    n = w.shape[1]
    tm, tn = min(tm, m), min(tn, n)
    return pl.pallas_call(
        functools.partial(_mm2_kernel, ka=ka),
        out_shape=jax.ShapeDtypeStruct((m, n), out_dtype),
        grid=(m // tm, n // tn),
        in_specs=[pl.BlockSpec((tm, ka), lambda i, j: (i, 0)),
                  pl.BlockSpec((tm, kb), lambda i, j: (i, 0)),
                  pl.BlockSpec((ka + kb, tn), lambda i, j: (0, j))],
        out_specs=pl.BlockSpec((tm, tn), lambda i, j: (i, j)),
        compiler_params=_cparams(("parallel", "arbitrary")),
        name="dense_matmul_2in",
    )(xa, xb, w)


def _layer_norm_rows(z, g, b):
    mu = jnp.mean(z, axis=-1, keepdims=True)
    zc = z - mu
    var = jnp.mean(zc * zc, axis=-1, keepdims=True)
    return zc * lax.rsqrt(var + LN_EPS) * g + b


def _res_ln_kernel(h_ref, mix_ref, g_ref, b_ref, o_ref, obf_ref):
    z = ALPHA * h_ref[...] + mix_ref[...]
    out = _layer_norm_rows(z, g_ref[...], b_ref[...])
    o_ref[...] = out
    obf_ref[...] = out.astype(BF16)


def _res_ln(h, mix, g, b, tb=256):
    t, d = h.shape
    tb = min(tb, t)
    row = pl.BlockSpec((tb, d), lambda i: (i, 0))
    vec = pl.BlockSpec((1, d), lambda i: (0, 0))
    return pl.pallas_call(
        _res_ln_kernel,
        out_shape=(jax.ShapeDtypeStruct((t, d), F32), jax.ShapeDtypeStruct((t, d), BF16)),
        grid=(t // tb,),
        in_specs=[row, row, vec, vec],
        out_specs=(row, row),
        compiler_params=_cparams(("parallel",)),
        name="residual_layernorm",
    )(h, mix, g.reshape(1, d), b.reshape(1, d))


def _split_dot(mat_bf, x):
    n = x.shape[1]
    hi = x.astype(BF16)
    lo = (x - hi.astype(F32)).astype(BF16)
    both = jnp.dot(mat_bf, jnp.concatenate([hi, lo], axis=1), preferred_element_type=F32)
    return both[:, :n] + both[:, n:]


def _hgrn_kernel(q_ref, ff_ref, fb_ref, v_ref, g_ref, lb_ref, ng_ref, o_ref, accf_ref, accb_ref, *, n_slabs):
    slab, chunk = HGRN_SLAB, HGRN_CHUNK
    n_chunks = slab // chunk
    row = lax.broadcasted_iota(I32, (slab, slab), 0)
    col = lax.broadcasted_iota(I32, (slab, slab), 1)
    same_chunk = (row // chunk) == (col // chunk)
    lb = lb_ref[0]

    def masks(reverse):
        keep = same_chunk & ((col >= row) if reverse else (col <= row))
        incl = jnp.where(keep, 1.0, 0.0).astype(BF16)
        return keep, incl

    def slab_pass(f_ref, acc_ref, si, state_t, reverse, keep, incl):
        r0 = pl.multiple_of(si * slab, slab)
        q = q_ref[pl.ds(r0, slab), :]
        v = v_ref[pl.ds(r0, slab), :]
        f = lb + (1.0 - lb) * jax.nn.sigmoid(f_ref[pl.ds(r0, slab), :])
        lf = jnp.log(f)
        kk = 1.0 - f
        spans = [(j * chunk, (j + 1) * chunk) for j in range(n_chunks)]
        b = _split_dot(incl, lf)
        tot = [b[(lo if reverse else hi - 1):(lo + 1 if reverse else hi), :] for lo, hi in spans]
        c = jnp.concatenate([jnp.broadcast_to(t, (chunk, A_DK)) for t in tot], axis=0) - b
        q_bf = (q * jnp.exp(b)).astype(BF16)
        k_dec = (kk * jnp.exp(-b)).astype(BF16)
        k_end = (kk * jnp.exp(c)).astype(BF16)
        v_bf = v.astype(BF16)
        att = lax.dot_general(q_bf, k_dec, (((1,), (1,)), ((), ())), preferred_element_type=F32)
        att = jnp.where(keep, att, 0.0).astype(BF16)
        o_intra = jnp.dot(att, v_bf, preferred_element_type=F32)
        upd = [lax.dot_general(v_bf[lo:hi], k_end[lo:hi], (((0,), (0,)), ((), ())),
                               preferred_element_type=F32) for lo, hi in spans]
        order = range(n_chunks - 1, -1, -1) if reverse else range(n_chunks)
        state_in = [None] * n_chunks
        for j in order:
            state_in[j] = state_t.astype(BF16)
            state_t = state_t * jnp.exp(tot[j]) + upd[j]
        for j in range(n_chunks):
            lo, hi = spans[j]
            o_inter = lax.dot_general(q_bf[lo:hi], state_in[j], (((1,), (1,)), ((), ())),
                                      preferred_element_type=F32)
            acc_ref[pl.ds(r0 + lo, chunk), :] = o_intra[lo:hi] + o_inter
        return state_t

    mask_f = masks(False)
    mask_b = masks(True)

    def body(i, states):
        st_f = slab_pass(ff_ref, accf_ref, i, states[0], False, *mask_f)
        st_b = slab_pass(fb_ref, accb_ref, n_slabs - 1 - i, states[1], True, *mask_b)
        return st_f, st_b

    zero = jnp.zeros((A_DK, A_DK), F32)
    lax.fori_loop(0, n_slabs, body, (zero, zero))

    ng = ng_ref[0]

    def finish(i, carry):
        r0 = pl.multiple_of(i * slab, slab)
        o = accf_ref[pl.ds(r0, slab), :] + accb_ref[pl.ds(r0, slab), :]
        o = o * lax.rsqrt(jnp.mean(o * o, axis=-1, keepdims=True) + RMS_EPS)
        g = g_ref[pl.ds(r0, slab), :]
        o_ref[pl.ds(r0, slab), :] = (o * ng * (g * jax.nn.sigmoid(g))).astype(o_ref.dtype)
        return carry

    lax.fori_loop(0, n_slabs, finish, 0)


def _hgrn(u, lb, norm_g, seq, batch):
    blk = lambda off: pl.BlockSpec((seq, 128), lambda b, h: (b, off + h))
    par = pl.BlockSpec((1, 1, 128), lambda b, h: (h, 0, 0))
    return pl.pallas_call(
        functools.partial(_hgrn_kernel, n_slabs=seq // HGRN_SLAB),
        out_shape=jax.ShapeDtypeStruct((batch * seq, A_WIDTH), BF16),
        grid=(batch, A_HEADS),
        in_specs=[blk(COL_QA), blk(COL_FF), blk(COL_FB), blk(COL_IA), blk(COL_GA), par, par],
        out_specs=pl.BlockSpec((seq, 128), lambda b, h: (b, h)),
        scratch_shapes=[pltpu.VMEM((seq, 128), F32), pltpu.VMEM((seq, 128), F32)],
        compiler_params=_cparams(("parallel", "parallel")),
        name="hgrn2_bidirectional",
    )(u, u, u, u, u, lb.reshape(A_HEADS, 1, A_DK), norm_g.reshape(A_HEADS, 1, 128))


def _t5_buckets(rel):
    nb = NUM_BUCKETS // 2
    ret = (rel > 0).astype(np.int32) * nb
    n = np.abs(rel)
    max_exact = nb // 2
    large = max_exact + (np.log(np.maximum(n, 1) / max_exact) / math.log(MAX_DISTANCE / max_exact)
                         * (nb - max_exact)).astype(np.int32)
    large = np.minimum(large, nb - 1)
    return ret + np.where(n < max_exact, n, large)


def _attn_kernel(q_ref, k_ref, v_ref, bias_ref, sink_ref, o_ref, kpad_ref, vpad_ref, *, seq):
    w = WINDOW
    nblk = seq // w
    zeros = jnp.zeros((w, B_HEAD_DIM), BF16)
    kpad_ref[0:w, :] = zeros
    vpad_ref[0:w, :] = zeros
    kpad_ref[w + seq:2 * w + seq, :] = zeros
    vpad_ref[w + seq:2 * w + seq, :] = zeros
    kpad_ref[w:w + seq, :] = k_ref[...].astype(BF16)
    vpad_ref[w:w + seq, :] = v_ref[...].astype(BF16)
    qi = lax.broadcasted_iota(I32, (w, 3 * w), 0)
    kj = lax.broadcasted_iota(I32, (w, 3 * w), 1)
    rel = kj - w - qi
    in_window = (rel <= w) & (rel >= -w)
    scale = B_HEAD_DIM ** -0.5

    bias_w = [jnp.where(in_window, bias_ref[g], NEG_INF) for g in range(B_GROUP)]

    def block(n, carry):
        r0 = pl.multiple_of(n * w, w)
        kabs = kj + (n - 1) * w
        in_seq = (kabs >= 0) & (kabs < seq)
        kb = kpad_ref[pl.ds(r0, 3 * w), :]
        vb = vpad_ref[pl.ds(r0, 3 * w), :]
        for g in range(B_GROUP):
            q = (q_ref[pl.ds(r0, w), g * B_HEAD_DIM:(g + 1) * B_HEAD_DIM] * scale).astype(BF16)
            s = lax.dot_general(q, kb, (((1,), (1,)), ((), ())), preferred_element_type=F32)
            s = jnp.where(in_seq, s + bias_w[g], NEG_INF)
            sk = sink_ref[g][:, 0:1]
            m = jnp.maximum(jnp.max(s, axis=-1, keepdims=True), sk)
            p = jnp.exp(s - m)
            den = jnp.sum(p, axis=-1, keepdims=True) + jnp.exp(sk - m)
            o = jnp.dot(p.astype(BF16), vb, preferred_element_type=F32) / den
            o_ref[pl.ds(r0, w), g * B_HEAD_DIM:(g + 1) * B_HEAD_DIM] = o.astype(o_ref.dtype)
        return carry

    lax.fori_loop(0, nblk, block, 0, unroll=2)


def _window_attention(u, sink, rel_table, seq, batch):
    rel = np.arange(-(2 * WINDOW - 1), 2 * WINDOW)
    per_rel = rel_table.astype(F32)[jnp.asarray(_t5_buckets(rel))].T
    bias = jnp.stack([per_rel[:, WINDOW - 1 - q:4 * WINDOW - 1 - q] for q in range(WINDOW)], axis=1)
    sink_b = jnp.broadcast_to(sink.astype(F32)[:, None, None], (B_Q_HEADS, 1, 128))
    qw = B_GROUP * B_HEAD_DIM
    return pl.pallas_call(
        functools.partial(_attn_kernel, seq=seq),
        out_shape=jax.ShapeDtypeStruct((batch * seq, B_WIDTH), BF16),
        grid=(batch, B_KV_HEADS),
        in_specs=[pl.BlockSpec((seq, qw), lambda b, h: (b, COL_QB_512 + h)),
                  pl.BlockSpec((seq, 128), lambda b, h: (b, COL_KB + h)),
                  pl.BlockSpec((seq, 128), lambda b, h: (b, COL_VB + h)),
                  pl.BlockSpec((B_GROUP, WINDOW, 3 * WINDOW), lambda b, h: (h, 0, 0)),
                  pl.BlockSpec((B_GROUP, 1, 128), lambda b, h: (h, 0, 0))],
        out_specs=pl.BlockSpec((seq, qw), lambda b, h: (b, h)),
        scratch_shapes=[pltpu.VMEM((seq + 2 * WINDOW, B_HEAD_DIM), BF16),
                        pltpu.VMEM((seq + 2 * WINDOW, B_HEAD_DIM), BF16)],
        compiler_params=_cparams(("parallel", "parallel")),
        name="window_attention",
    )(u, u, u, bias, sink_b)


def _gelu_tanh(x):
    return 0.5 * x * (1.0 + jnp.tanh(math.sqrt(2.0 / math.pi) * (x + 0.044715 * x * x * x)))


def _rglru_kernel(xr_ref, prev_ref, next_ref, y_ref, cw_ref, cb_ref, wr_ref, br_ref, wi_ref, bi_ref,
                  sp_ref, o_ref, hsf_ref, a_ref, d_ref, hs_ref, h_ref, *, ts, n_t, batch):
    p = pl.program_id(1)
    t = pl.program_id(2)
    tb = t + p * (n_t - 1 - 2 * t)

    @pl.when(t == 0)
    def _():
        h_ref[...] = jnp.zeros_like(h_ref)

    xr = xr_ref[...]
    prev = jnp.where(tb == 0, 0.0, prev_ref[...])
    nxt = jnp.where(tb == n_t - 1, 0.0, next_ref[...])
    xfull = jnp.concatenate([prev, xr, nxt], axis=0)
    cw = cw_ref[...]
    xc = cb_ref[...].reshape(1, 1, C_BLOCK) + sum(
        xfull[j:j + ts] * cw[j:j + 1, :].reshape(1, 1, C_BLOCK) for j in range(C_CONV))
    x2 = xc.reshape(ts * batch, C_BLOCK)
    xb = x2.astype(BF16)
    r = jax.nn.sigmoid(jnp.dot(xb, wr_ref[...], preferred_element_type=F32) + br_ref[0])
    gi = jax.nn.sigmoid(jnp.dot(xb, wi_ref[...], preferred_element_type=F32) + bi_ref[0])
    log_a = (-RG_C) * r * sp_ref[0]
    a = jnp.exp(log_a)
    drive = jnp.sqrt(1.0 - a * a) * (gi * x2)
    a_ref[...] = a.reshape(ts, batch, C_BLOCK)
    d_ref[...] = drive.reshape(ts, batch, C_BLOCK)

    def step(i, h):
        idx = i + p * (ts - 1 - 2 * i)
        h = a_ref[idx] * h + d_ref[idx]
        hs_ref[idx] = h
        return h

    h_ref[...] = lax.fori_loop(0, ts, step, h_ref[...], unroll=8)
    base = pl.multiple_of(tb * ts, ts)

    @pl.when(p == 0)
    def _():
        hsf_ref[pl.ds(base, ts)] = hs_ref[...]

    @pl.when(p == 1)
    def _():
        hs = hsf_ref[pl.ds(base, ts)] + hs_ref[...]
        o_ref[...] = (hs.reshape(ts * batch, C_BLOCK) * _gelu_tanh(y_ref[...])).astype(o_ref.dtype)


def _rglru(u, conv_w, conv_b, w_r, b_r, w_i, b_i, a_param, seq, batch):
    ts = min(256, seq)
    n_t = seq // ts
    nc = C_GATE_BLOCKS
    u3 = u.reshape(seq, batch, 2 * C_WIDTH)
    sp = jax.nn.softplus(-a_param.astype(F32)).reshape(2, nc, 1, C_BLOCK)
    tblk = lambda p, t: t + p * (n_t - 1 - 2 * t)
    pinned = lambda p, t: p * tblk(p, t) + (1 - p) * (n_t - 1)
    per_dir = lambda shape: pl.BlockSpec((None, None) + shape, lambda c, p, t: (p, c, 0, 0))
    kern = functools.partial(_rglru_kernel, ts=ts, n_t=n_t, batch=batch)
    return pl.pallas_call(
        kern,
        out_shape=jax.ShapeDtypeStruct((seq * batch, C_WIDTH), BF16),
        grid=(nc, 2, n_t),
        in_specs=[
            pl.BlockSpec((ts, batch, C_BLOCK), lambda c, p, t: (tblk(p, t), 0, nc + c)),
            pl.BlockSpec((2, batch, C_BLOCK),
                         lambda c, p, t: (jnp.maximum(tblk(p, t) * (ts // 2) - 1, 0), 0, nc + c)),
            pl.BlockSpec((1, batch, C_BLOCK),
                         lambda c, p, t: (jnp.minimum((tblk(p, t) + 1) * ts, seq - 1), 0, nc + c)),
            pl.BlockSpec((ts * batch, C_BLOCK), lambda c, p, t: (pinned(p, t), c)),
            pl.BlockSpec((C_CONV, C_BLOCK), lambda c, p, t: (0, c)),
            pl.BlockSpec((1, C_BLOCK), lambda c, p, t: (0, c)),
            per_dir((C_BLOCK, C_BLOCK)), per_dir((1, C_BLOCK)),
            per_dir((C_BLOCK, C_BLOCK)), per_dir((1, C_BLOCK)),
            per_dir((1, C_BLOCK)),
        ],
        out_specs=pl.BlockSpec((ts * batch, C_BLOCK), lambda c, p, t: (pinned(p, t), c)),
        scratch_shapes=[pltpu.VMEM((seq, batch, C_BLOCK), F32),
                        pltpu.VMEM((ts, batch, C_BLOCK), F32),
                        pltpu.VMEM((ts, batch, C_BLOCK), F32),
                        pltpu.VMEM((ts, batch, C_BLOCK), F32),
                        pltpu.VMEM((batch, C_BLOCK), F32)],
        compiler_params=_cparams(("parallel", "arbitrary", "arbitrary")),
        name="rglru_bidirectional",
    )(u3, u3, u3, u, conv_w, conv_b.reshape(1, C_WIDTH),
      w_r.astype(BF16), b_r.reshape(2, nc, 1, C_BLOCK), w_i.astype(BF16), b_i.reshape(2, nc, 1, C_BLOCK), sp)


def _route_kernel(h_ref, rwt_ref, rb_ref, e1_ref, e2_ref, g1_ref, g2_ref, r1_ref, r2_ref, cnt_ref,
                  carry_ref, *, tb):
    @pl.when(pl.program_id(0) == 0)
    def _():
        carry_ref[...] = jnp.zeros_like(carry_ref)

    logits = lax.dot_general(rwt_ref[...], h_ref[...], (((1,), (1,)), ((), ())),
                             precision=lax.Precision.HIGHEST,
                             preferred_element_type=F32) + rb_ref[...]
    mx = jnp.max(logits, axis=0, keepdims=True)
    ex = jnp.exp(logits - mx)
    probs = ex / jnp.sum(ex, axis=0, keepdims=True)
    eid = lax.broadcasted_iota(I32, (N_EXPERTS, tb), 0)
    gmax = jnp.max(probs.reshape(N_GROUPS, EXPERTS_PER_GROUP, tb), axis=1)
    gid = lax.broadcasted_iota(I32, (N_GROUPS, tb), 0)
    grp = jnp.min(jnp.where(gmax == jnp.max(gmax, axis=0, keepdims=True), gid, N_GROUPS),
                  axis=0, keepdims=True)
    sel = jnp.where((eid // EXPERTS_PER_GROUP) == grp, probs, -1.0)
    m1 = jnp.max(sel, axis=0, keepdims=True)
    e1 = jnp.min(jnp.where(sel == m1, eid, N_EXPERTS), axis=0, keepdims=True)
    sel2 = jnp.where(eid == e1, -1.0, sel)
    m2 = jnp.max(sel2, axis=0, keepdims=True)
    e2 = jnp.min(jnp.where(sel2 == m2, eid, N_EXPERTS), axis=0, keepdims=True)
    tot = m1 + m2
    e1_ref[...] = e1
    e2_ref[...] = e2
    g1_ref[...] = m1 / tot
    g2_ref[...] = m2 / tot
    oh1 = eid == e1
    oh2 = eid == e2
    oh = jnp.where(oh1 | oh2, 1.0, 0.0)
    before = jnp.where(lax.broadcasted_iota(I32, (tb, tb), 0) < lax.broadcasted_iota(I32, (tb, tb), 1),
                       1.0, 0.0).astype(BF16)
    base = carry_ref[...] + jnp.dot(oh.astype(BF16), before, preferred_element_type=F32)
    r1_ref[...] = jnp.sum(jnp.where(oh1, base, 0.0), axis=0, keepdims=True).astype(I32)
    r2_ref[...] = jnp.sum(jnp.where(oh2, base, 0.0), axis=0, keepdims=True).astype(I32)
    carry_ref[...] = carry_ref[...] + jnp.sum(oh, axis=1, keepdims=True)
    cnt_ref[...] = jnp.broadcast_to(carry_ref[...], cnt_ref.shape).astype(I32)


def _route(h, router_w, router_b, tb=512):
    t, d = h.shape
    tb = min(tb, t)
    lane = lambda dt: jax.ShapeDtypeStruct((1, t), dt)
    lspec = pl.BlockSpec((1, tb), lambda i: (0, i))
    outs = pl.pallas_call(
        functools.partial(_route_kernel, tb=tb),
        out_shape=(lane(I32), lane(I32), lane(F32), lane(F32), lane(I32), lane(I32),
                   jax.ShapeDtypeStruct((N_EXPERTS, 128), I32)),
        grid=(t // tb,),
        in_specs=[pl.BlockSpec((tb, d), lambda i: (i, 0)),
                  pl.BlockSpec((N_EXPERTS, d), lambda i: (0, 0)),
                  pl.BlockSpec((N_EXPERTS, 1), lambda i: (0, 0))],
        out_specs=(lspec, lspec, lspec, lspec, lspec, lspec,
                   pl.BlockSpec((N_EXPERTS, 128), lambda i: (0, 0))),
        scratch_shapes=[pltpu.VMEM((N_EXPERTS, 1), F32)],
        compiler_params=_cparams(("arbitrary",)),
        name="moe_route",
    )(h, router_w.T, router_b.reshape(N_EXPERTS, 1).astype(F32))
    e1, e2, g1, g2, r1, r2, cnt = outs
    return e1[0], e2[0], g1[0], g2[0], r1[0], r2[0], cnt[:, 0]


def _row_copy(src_ref, src_row, dst_ref, dst_row, sem):
    return pltpu.make_async_copy(src_ref.at[pl.ds(src_row, 1)], dst_ref.at[pl.ds(dst_row, 1)], sem)


def _dispatch_kernel(d1_ref, d2_ref, pad0_ref, padn_ref, h_ref, xs_ref, zero_ref, sems, *, tb):
    t0 = pl.program_id(0) * tb

    @pl.when(pl.program_id(0) == 0)
    def _():
        zero_ref[...] = jnp.zeros_like(zero_ref)

        def fill(wait):
            def segment(e, carry):
                def body(r, c):
                    cp = _row_copy(zero_ref, 0, xs_ref, pad0_ref[e] + r, sems.at[2])
                    cp.wait() if wait else cp.start()
                    return c
                return lax.fori_loop(0, padn_ref[e], body, carry)
            lax.fori_loop(0, N_EXPERTS + 1, segment, 0)

        fill(False)
        fill(True)

    def start(r, carry):
        _row_copy(h_ref, r, xs_ref, d1_ref[t0 + r], sems.at[0]).start()
        _row_copy(h_ref, r, xs_ref, d2_ref[t0 + r], sems.at[1]).start()
        return carry

    def wait(r, carry):
        _row_copy(h_ref, r, xs_ref, d1_ref[t0 + r], sems.at[0]).wait()
        _row_copy(h_ref, r, xs_ref, d2_ref[t0 + r], sems.at[1]).wait()
        return carry

    lax.fori_loop(0, tb, start, 0)
    lax.fori_loop(0, tb, wait, 0)


def _dispatch(h, dest1, dest2, pad_start, pad_count, n_rows, tb=128):
    t, d = h.shape
    tb = min(tb, t)
    return pl.pallas_call(
        functools.partial(_dispatch_kernel, tb=tb),
        out_shape=jax.ShapeDtypeStruct((n_rows, d), F32),
        grid_spec=pltpu.PrefetchScalarGridSpec(
            num_scalar_prefetch=4,
            grid=(t // tb,),
            in_specs=[pl.BlockSpec((tb, d), lambda i, *_: (i, 0))],
            out_specs=pl.BlockSpec(memory_space=pl.ANY),
            scratch_shapes=[pltpu.VMEM((8, d), F32), pltpu.SemaphoreType.DMA((3,))],
        ),
        compiler_params=_cparams(("arbitrary",)),
        name="moe_dispatch",
    )(dest1, dest2, pad_start, pad_count, h)


def _expert_kernel(be_ref, nu_ref, x_ref, wg_ref, wu_ref, wd_ref, o_ref):
    b = pl.program_id(0)

    @pl.when(b < nu_ref[0])
    def _():
        x = x_ref[...].astype(BF16)
        hg = jnp.dot(x, wg_ref[...], preferred_element_type=F32)
        hu = jnp.dot(x, wu_ref[...], preferred_element_type=F32)
        hid = (hg * jax.nn.sigmoid(hg) * hu).astype(BF16)
        o_ref[...] = jnp.dot(hid, wd_ref[...], preferred_element_type=F32)

    @pl.when(b >= nu_ref[0])
    def _():
        o_ref[...] = jnp.zeros_like(o_ref)


def _experts(xs, block_e, n_used, w_gate, w_up, w_down, layer):
    p, d = xs.shape
    nblk = p // MOE_BLOCK
    de = w_gate.shape[-1]
    xrow = lambda b, be, nu: (jnp.minimum(b, jnp.maximum(nu[0] - 1, 0)), 0)
    wsel = lambda b, be, nu: (layer, be[b], 0, 0)
    return pl.pallas_call(
        _expert_kernel,
        out_shape=jax.ShapeDtypeStruct((p, d), F32),
        grid_spec=pltpu.PrefetchScalarGridSpec(
            num_scalar_prefetch=2,
            grid=(nblk,),
            in_specs=[pl.BlockSpec((MOE_BLOCK, d), xrow),
                      pl.BlockSpec((None, None, d, de), wsel),
                      pl.BlockSpec((None, None, d, de), wsel),
                      pl.BlockSpec((None, None, de, d), wsel)],
            out_specs=pl.BlockSpec((MOE_BLOCK, d), lambda b, be, nu: (b, 0)),
        ),
        compiler_params=_cparams(("arbitrary",)),
        name="moe_experts",
    )(block_e, n_used, xs, w_gate, w_up, w_down)


def _combine_ln_kernel(d1_ref, d2_ref, yb_ref, h_ref, g1_ref, g2_ref, lg_ref, lb_ref, o_ref, obf_ref,
                       y1_ref, y2_ref, sems, *, tb):
    i = pl.program_id(0)
    slot = i % 2

    def copies(blk, s, r):
        t = blk * tb + r
        return (_row_copy(yb_ref, d1_ref[t], y1_ref.at[s], r, sems.at[s, 0]),
                _row_copy(yb_ref, d2_ref[t], y2_ref.at[s], r, sems.at[s, 1]))

    def gather(blk, s, wait):
        def body(r, carry):
            for c in copies(blk, s, r):
                c.wait() if wait else c.start()
            return carry
        lax.fori_loop(0, tb, body, 0)

    @pl.when(i == 0)
    def _():
        gather(0, 0, False)

    @pl.when(i + 1 < pl.num_programs(0))
    def _():
        gather(i + 1, 1 - slot, False)

    gather(i, slot, True)
    y = y1_ref[slot] * g1_ref[...] + y2_ref[slot] * g2_ref[...]
    out = _layer_norm_rows(ALPHA * h_ref[...] + y, lg_ref[...], lb_ref[...])
    o_ref[...] = out
    obf_ref[...] = out.astype(BF16)


def _combine_ln(yb, h, dest1, dest2, g1, g2, ln_g, ln_b, tb=128):
    t, d = h.shape
    tb = min(tb, t)
    row = pl.BlockSpec((tb, d), lambda i, d1, d2: (i, 0))
    colv = pl.BlockSpec((tb, 1), lambda i, d1, d2: (i, 0))
    vec = pl.BlockSpec((1, d), lambda i, d1, d2: (0, 0))
    return pl.pallas_call(
        functools.partial(_combine_ln_kernel, tb=tb),
        out_shape=(jax.ShapeDtypeStruct((t, d), F32), jax.ShapeDtypeStruct((t, d), BF16)),
        grid_spec=pltpu.PrefetchScalarGridSpec(
            num_scalar_prefetch=2,
            grid=(t // tb,),
            in_specs=[pl.BlockSpec(memory_space=pl.ANY), row, colv, colv, vec, vec],
            out_specs=(row, row),
            scratch_shapes=[pltpu.VMEM((2, tb, d), F32), pltpu.VMEM((2, tb, d), F32),
                            pltpu.SemaphoreType.DMA((2, 2))],
        ),
        compiler_params=_cparams(("arbitrary",)),
        name="moe_combine_layernorm",
    )(dest1, dest2, yb, h, g1.reshape(t, 1), g2.reshape(t, 1), ln_g.reshape(1, d), ln_b.reshape(1, d))


def _moe_ln(h, router_w, router_b, w_gate, w_up, w_down, layer, ln_g, ln_b):
    t, d = h.shape
    e1, e2, g1, g2, r1, r2, counts = _route(h, router_w, router_b)
    padded = (counts + MOE_BLOCK - 1) // MOE_BLOCK * MOE_BLOCK
    pends = jnp.cumsum(padded)
    pstarts = pends - padded
    dest1 = pstarts[e1] + r1
    dest2 = pstarts[e2] + r2
    nblk = -(-(t * TOP_K) // MOE_BLOCK) + N_EXPERTS
    n_used = (pends[-1] // MOE_BLOCK).astype(I32)
    blk_row = jnp.minimum(jnp.arange(nblk, dtype=I32), jnp.maximum(n_used - 1, 0)) * MOE_BLOCK
    block_e = jnp.minimum(jnp.sum(pends[None, :] <= blk_row[:, None], axis=1), N_EXPERTS - 1).astype(I32)
    n_rows = nblk * MOE_BLOCK
    pad_start = jnp.concatenate([pstarts + counts, pends[-1:]]).astype(I32)
    pad_count = jnp.concatenate([padded - counts, n_rows - pends[-1:]]).astype(I32)
    xs = _dispatch(h, dest1, dest2, pad_start, pad_count, n_rows)
    yb = _experts(xs, block_e, n_used.reshape(1), w_gate, w_up, w_down, layer)
    return _combine_ln(yb, h, dest1, dest2, g1, g2, ln_g, ln_b)


def _swap_token_order(a, outer, inner):
    return jnp.transpose(a.reshape(outer, inner, a.shape[-1]), (1, 0, 2)).reshape(outer * inner, a.shape[-1])


def kernel(x, ab_w_in, ab_w_out, hgrn_lower_bounds, hgrn_norm_g, attn_sink, rel_bias_table, c_w_in, c_conv_w, c_conv_b, c_w_rgate, c_b_rgate, c_w_igate, c_b_igate, c_a_param, c_w_out, ln_g, ln_b, router_w, router_b, moe_w_gate, moe_w_up, moe_w_down):
    batch, seq, d = x.shape
    t = batch * seq
    lb_all = jnp.cumsum(jax.nn.softmax(hgrn_lower_bounds.astype(F32), axis=0), axis=0)
    h = x.reshape(t, d)
    h_bf = h.astype(BF16)
    wg_bf, wu_bf, wd_bf = moe_w_gate.astype(BF16), moe_w_up.astype(BF16), moe_w_down.astype(BF16)
    batch_major = True
    for l in range(DEPTH):
        j = l // 2
        if (l % 2 == 0) != batch_major:
            outer, inner = (batch, seq) if batch_major else (seq, batch)
            h, h_bf = _swap_token_order(h, outer, inner), _swap_token_order(h_bf, outer, inner)
            batch_major = not batch_major
        if l % 2 == 0:
            u = _matmul(h_bf, ab_w_in[j].astype(BF16), F32, 1024, 1024)
            o_a = _hgrn(u, lb_all[l], hgrn_norm_g[j], seq, batch)
            o_b = _window_attention(u, attn_sink[j], rel_bias_table, seq, batch)
            mix = _matmul2(o_a, o_b, ab_w_out[j].astype(BF16), F32, 1024, 1024)
        else:
            u = _matmul(h_bf, c_w_in[j].astype(BF16), F32, 1024, 1024)
            z = _rglru(u, c_conv_w[j], c_conv_b[j], c_w_rgate[j], c_b_rgate[j], c_w_igate[j], c_b_igate[j],
                       c_a_param[j], seq, batch)
            mix = _matmul(z, c_w_out[j].astype(BF16), F32, 1024, 1024)
        h, _ = _res_ln(h, mix, ln_g[l, 0], ln_b[l, 0])
        h, h_bf = _moe_ln(h, router_w, router_b, wg_bf, wu_bf, wd_bf, l, ln_g[l, 1], ln_b[l, 1])
    if not batch_major:
        h = _swap_token_order(h, seq, batch)
    return h.reshape(batch, seq, d)
```

```python
import functools
import math

import numpy as np
import jax
import jax.numpy as jnp
from jax import lax
from jax.experimental import pallas as pl
from jax.experimental.pallas import tpu as pltpu

F32 = jnp.float32
BF16 = jnp.bfloat16
I32 = jnp.int32

D_MODEL = 4096
DEPTH = 2
A_HEADS = 16
A_DK = 128
A_WIDTH = 2048
HGRN_CHUNK = 32
HGRN_SLAB = 256
B_Q_HEADS = 16
B_KV_HEADS = 4
B_GROUP = B_Q_HEADS // B_KV_HEADS
B_HEAD_DIM = 128
B_WIDTH = 2048
WINDOW = 128
NUM_BUCKETS = 32
MAX_DISTANCE = 128
NEG_INF = -1e30
AB_IN = 13312
COL_QA, COL_FF, COL_FB, COL_IA, COL_GA = 0, 16, 32, 48, 64
COL_QB_512, COL_KB, COL_VB = 20, 96, 100
C_WIDTH = 4096
C_BLOCK = 256
C_GATE_BLOCKS = 16
C_CONV = 4
RG_C = 8.0
N_EXPERTS = 32
N_GROUPS = 4
EXPERTS_PER_GROUP = 8
TOP_K = 2
D_EXPERT = 768
MOE_BLOCK = 128
ALPHA = (2 * DEPTH) ** 0.25
LN_EPS = 1e-5
RMS_EPS = 1e-6

VMEM_LIMIT = 56 * 1024 * 1024


def _cparams(sem, vmem=VMEM_LIMIT):
    return pltpu.CompilerParams(dimension_semantics=sem, vmem_limit_bytes=vmem)


def _mm_kernel(x_ref, w_ref, o_ref):
    o_ref[...] = jnp.dot(x_ref[...], w_ref[...], preferred_element_type=F32).astype(o_ref.dtype)


def _matmul(x, w, out_dtype, tm, tn):
    m, k = x.shape
    n = w.shape[1]
    tm, tn = min(tm, m), min(tn, n)
    return pl.pallas_call(
        _mm_kernel,
        out_shape=jax.ShapeDtypeStruct((m, n), out_dtype),
        grid=(m // tm, n // tn),
        in_specs=[pl.BlockSpec((tm, k), lambda i, j: (i, 0)),
                  pl.BlockSpec((k, tn), lambda i, j: (0, j))],
        out_specs=pl.BlockSpec((tm, tn), lambda i, j: (i, j)),
        compiler_params=_cparams(("parallel", "arbitrary")),
        name="dense_matmul",
    )(x, w)


def _mm2_kernel(xa_ref, xb_ref, w_ref, o_ref, *, ka):
    acc = jnp.dot(xa_ref[...], w_ref[0:ka, :], preferred_element_type=F32)
    acc = acc + jnp.dot(xb_ref[...], w_ref[ka:, :], preferred_element_type=F32)
    o_ref[...] = acc.astype(o_ref.dtype)


def _matmul2(xa, xb, w, out_dtype, tm, tn):
    m, ka = xa.shape
    kb = xb.shape[1]
    n = w.shape[1]
    tm, tn = min(tm, m), min(tn, n)
    return pl.pallas_call(
        functools.partial(_mm2_kernel, ka=ka),
        out_shape=jax.ShapeDtypeStruct((m, n), out_dtype),
        grid=(m // tm, n // tn),
        in_specs=[pl.BlockSpec((tm, ka), lambda i, j: (i, 0)),
                  pl.BlockSpec((tm, kb), lambda i, j: (i, 0)),
                  pl.BlockSpec((ka + kb, tn), lambda i, j: (0, j))],
        out_specs=pl.BlockSpec((tm, tn), lambda i, j: (i, j)),
        compiler_params=_cparams(("parallel", "arbitrary")),
        name="dense_matmul_2in",
    )(xa, xb, w)


def _layer_norm_rows(z, g, b):
    mu = jnp.mean(z, axis=-1, keepdims=True)
    zc = z - mu
    var = jnp.mean(zc * zc, axis=-1, keepdims=True)
    return zc * lax.rsqrt(var + LN_EPS) * g + b


def _res_ln_kernel(h_ref, mix_ref, g_ref, b_ref, o_ref, obf_ref):
    z = ALPHA * h_ref[...] + mix_ref[...]
    out = _layer_norm_rows(z, g_ref[...], b_ref[...])
    o_ref[...] = out
    obf_ref[...] = out.astype(BF16)


def _res_ln(h, mix, g, b, tb=256):
    t, d = h.shape
    tb = min(tb, t)
    row = pl.BlockSpec((tb, d), lambda i: (i, 0))
    vec = pl.BlockSpec((1, d), lambda i: (0, 0))
    return pl.pallas_call(
        _res_ln_kernel,
        out_shape=(jax.ShapeDtypeStruct((t, d), F32), jax.ShapeDtypeStruct((t, d), BF16)),
        grid=(t // tb,),
        in_specs=[row, row, vec, vec],
        out_specs=(row, row),
        compiler_params=_cparams(("parallel",)),
        name="residual_layernorm",
    )(h, mix, g.reshape(1, d), b.reshape(1, d))


def _split_dot(mat_bf, x):
    n = x.shape[1]
    hi = x.astype(BF16)
    lo = (x - hi.astype(F32)).astype(BF16)
    both = jnp.dot(mat_bf, jnp.concatenate([hi, lo], axis=1), preferred_element_type=F32)
    return both[:, :n] + both[:, n:]


def _hgrn_kernel(q_ref, ff_ref, fb_ref, v_ref, g_ref, lb_ref, ng_ref, o_ref, accf_ref, accb_ref, *, n_slabs):
    slab, chunk = HGRN_SLAB, HGRN_CHUNK
    n_chunks = slab // chunk
    row = lax.broadcasted_iota(I32, (slab, slab), 0)
    col = lax.broadcasted_iota(I32, (slab, slab), 1)
    same_chunk = (row // chunk) == (col // chunk)
    lb = lb_ref[0]

    def masks(reverse):
        keep = same_chunk & ((col >= row) if reverse else (col <= row))
        incl = jnp.where(keep, 1.0, 0.0).astype(BF16)
        return keep, incl

    def slab_pass(f_ref, acc_ref, si, state_t, reverse, keep, incl):
        r0 = pl.multiple_of(si * slab, slab)
        q = q_ref[pl.ds(r0, slab), :]
        v = v_ref[pl.ds(r0, slab), :]
        f = lb + (1.0 - lb) * jax.nn.sigmoid(f_ref[pl.ds(r0, slab), :])
        lf = jnp.log(f)
        kk = 1.0 - f
        spans = [(j * chunk, (j + 1) * chunk) for j in range(n_chunks)]
        b = _split_dot(incl, lf)
        tot = [b[(lo if reverse else hi - 1):(lo + 1 if reverse else hi), :] for lo, hi in spans]
        c = jnp.concatenate([jnp.broadcast_to(t, (chunk, A_DK)) for t in tot], axis=0) - b
        q_bf = (q * jnp.exp(b)).astype(BF16)
        k_dec = (kk * jnp.exp(-b)).astype(BF16)
        k_end = (kk * jnp.exp(c)).astype(BF16)
        v_bf = v.astype(BF16)
        att = lax.dot_general(q_bf, k_dec, (((1,), (1,)), ((), ())), preferred_element_type=F32)
        att = jnp.where(keep, att, 0.0).astype(BF16)
        o_intra = jnp.dot(att, v_bf, preferred_element_type=F32)
        upd = [lax.dot_general(v_bf[lo:hi], k_end[lo:hi], (((0,), (0,)), ((), ())),
                               preferred_element_type=F32) for lo, hi in spans]
        order = range(n_chunks - 1, -1, -1) if reverse else range(n_chunks)
        state_in = [None] * n_chunks
        for j in order:
            state_in[j] = state_t.astype(BF16)
            state_t = state_t * jnp.exp(tot[j]) + upd[j]
        for j in range(n_chunks):
            lo, hi = spans[j]
            o_inter = lax.dot_general(q_bf[lo:hi], state_in[j], (((1,), (1,)), ((), ())),
                                      preferred_element_type=F32)
            acc_ref[pl.ds(r0 + lo, chunk), :] = o_intra[lo:hi] + o_inter
        return state_t

    mask_f = masks(False)
    mask_b = masks(True)

    def body(i, states):
        st_f = slab_pass(ff_ref, accf_ref, i, states[0], False, *mask_f)
        st_b = slab_pass(fb_ref, accb_ref, n_slabs - 1 - i, states[1], True, *mask_b)
        return st_f, st_b

    zero = jnp.zeros((A_DK, A_DK), F32)
    lax.fori_loop(0, n_slabs, body, (zero, zero), unroll=2)

    ng = ng_ref[0]

    def finish(i, carry):
        r0 = pl.multiple_of(i * slab, slab)
        o = accf_ref[pl.ds(r0, slab), :] + accb_ref[pl.ds(r0, slab), :]
        o = o * lax.rsqrt(jnp.mean(o * o, axis=-1, keepdims=True) + RMS_EPS)
        g = g_ref[pl.ds(r0, slab), :]
        o_ref[pl.ds(r0, slab), :] = (o * ng * (g * jax.nn.sigmoid(g))).astype(o_ref.dtype)
        return carry

    lax.fori_loop(0, n_slabs, finish, 0)


def _hgrn(u, lb, norm_g, seq, batch):
    blk = lambda off: pl.BlockSpec((seq, 128), lambda b, h: (b, off + h))
    par = pl.BlockSpec((1, 1, 128), lambda b, h: (h, 0, 0))
    return pl.pallas_call(
        functools.partial(_hgrn_kernel, n_slabs=seq // HGRN_SLAB),
        out_shape=jax.ShapeDtypeStruct((batch * seq, A_WIDTH), BF16),
        grid=(batch, A_HEADS),
        in_specs=[blk(COL_QA), blk(COL_FF), blk(COL_FB), blk(COL_IA), blk(COL_GA), par, par],
        out_specs=pl.BlockSpec((seq, 128), lambda b, h: (b, h)),
        scratch_shapes=[pltpu.VMEM((seq, 128), F32), pltpu.VMEM((seq, 128), F32)],
        compiler_params=_cparams(("parallel", "parallel")),
        name="hgrn2_bidirectional",
    )(u, u, u, u, u, lb.reshape(A_HEADS, 1, A_DK), norm_g.reshape(A_HEADS, 1, 128))


def _t5_buckets(rel):
    nb = NUM_BUCKETS // 2
    ret = (rel > 0).astype(np.int32) * nb
    n = np.abs(rel)
    max_exact = nb // 2
    large = max_exact + (np.log(np.maximum(n, 1) / max_exact) / math.log(MAX_DISTANCE / max_exact)
                         * (nb - max_exact)).astype(np.int32)
    large = np.minimum(large, nb - 1)
    return ret + np.where(n < max_exact, n, large)


def _attn_kernel(q_ref, k_ref, v_ref, bias_ref, sink_ref, o_ref, kpad_ref, vpad_ref, *, seq):
    w = WINDOW
    nblk = seq // w
    zeros = jnp.zeros((w, B_HEAD_DIM), BF16)
    kpad_ref[0:w, :] = zeros
    vpad_ref[0:w, :] = zeros
    kpad_ref[w + seq:2 * w + seq, :] = zeros
    vpad_ref[w + seq:2 * w + seq, :] = zeros
    kpad_ref[w:w + seq, :] = k_ref[...].astype(BF16)
    vpad_ref[w:w + seq, :] = v_ref[...].astype(BF16)
    qi = lax.broadcasted_iota(I32, (w, 3 * w), 0)
    kj = lax.broadcasted_iota(I32, (w, 3 * w), 1)
    rel = kj - w - qi
    in_window = (rel <= w) & (rel >= -w)
    scale = B_HEAD_DIM ** -0.5

    bias_w = [jnp.where(in_window, bias_ref[g], NEG_INF) for g in range(B_GROUP)]

    def block(n, carry):
        r0 = pl.multiple_of(n * w, w)
        kabs = kj + (n - 1) * w
        in_seq = (kabs >= 0) & (kabs < seq)
        kb = kpad_ref[pl.ds(r0, 3 * w), :]
        vb = vpad_ref[pl.ds(r0, 3 * w), :]
        for g in range(B_GROUP):
            q = (q_ref[pl.ds(r0, w), g * B_HEAD_DIM:(g + 1) * B_HEAD_DIM] * scale).astype(BF16)
            s = lax.dot_general(q, kb, (((1,), (1,)), ((), ())), preferred_element_type=F32)
            s = jnp.where(in_seq, s + bias_w[g], NEG_INF)
            sk = sink_ref[g][:, 0:1]
            m = jnp.maximum(jnp.max(s, axis=-1, keepdims=True), sk)
            p = jnp.exp(s - m)
            den = jnp.sum(p, axis=-1, keepdims=True) + jnp.exp(sk - m)
            o = jnp.dot(p.astype(BF16), vb, preferred_element_type=F32) / den
            o_ref[pl.ds(r0, w), g * B_HEAD_DIM:(g + 1) * B_HEAD_DIM] = o.astype(o_ref.dtype)
        return carry

    lax.fori_loop(0, nblk, block, 0, unroll=2)


def _window_attention(u, sink, rel_table, seq, batch):
    rel = np.arange(-(2 * WINDOW - 1), 2 * WINDOW)
    per_rel = rel_table.astype(F32)[jnp.asarray(_t5_buckets(rel))].T
    bias = jnp.stack([per_rel[:, WINDOW - 1 - q:4 * WINDOW - 1 - q] for q in range(WINDOW)], axis=1)
    sink_b = jnp.broadcast_to(sink.astype(F32)[:, None, None], (B_Q_HEADS, 1, 128))
    qw = B_GROUP * B_HEAD_DIM
    return pl.pallas_call(
        functools.partial(_attn_kernel, seq=seq),
        out_shape=jax.ShapeDtypeStruct((batch * seq, B_WIDTH), BF16),
        grid=(batch, B_KV_HEADS),
        in_specs=[pl.BlockSpec((seq, qw), lambda b, h: (b, COL_QB_512 + h)),
                  pl.BlockSpec((seq, 128), lambda b, h: (b, COL_KB + h)),
                  pl.BlockSpec((seq, 128), lambda b, h: (b, COL_VB + h)),
                  pl.BlockSpec((B_GROUP, WINDOW, 3 * WINDOW), lambda b, h: (h, 0, 0)),
                  pl.BlockSpec((B_GROUP, 1, 128), lambda b, h: (h, 0, 0))],
        out_specs=pl.BlockSpec((seq, qw), lambda b, h: (b, h)),
        scratch_shapes=[pltpu.VMEM((seq + 2 * WINDOW, B_HEAD_DIM), BF16),
                        pltpu.VMEM((seq + 2 * WINDOW, B_HEAD_DIM), BF16)],
        compiler_params=_cparams(("parallel", "parallel")),
        name="window_attention",
    )(u, u, u, bias, sink_b)


def _gelu_tanh(x):
    return 0.5 * x * (1.0 + jnp.tanh(math.sqrt(2.0 / math.pi) * (x + 0.044715 * x * x * x)))


def _rglru_kernel(xr_ref, prev_ref, next_ref, y_ref, cw_ref, cb_ref, wr_ref, br_ref, wi_ref, bi_ref,
                  sp_ref, o_ref, hsf_ref, a_ref, d_ref, hs_ref, h_ref, *, ts, n_t, batch):
    p = pl.program_id(1)
    t = pl.program_id(2)
    tb = t + p * (n_t - 1 - 2 * t)

    @pl.when(t == 0)
    def _():
        h_ref[...] = jnp.zeros_like(h_ref)

    xr = xr_ref[...]
    prev = jnp.where(tb == 0, 0.0, prev_ref[...])
    nxt = jnp.where(tb == n_t - 1, 0.0, next_ref[...])
    xfull = jnp.concatenate([prev, xr, nxt], axis=0)
    cw = cw_ref[...]
    xc = cb_ref[...].reshape(1, 1, C_BLOCK) + sum(
        xfull[j:j + ts] * cw[j:j + 1, :].reshape(1, 1, C_BLOCK) for j in range(C_CONV))
    x2 = xc.reshape(ts * batch, C_BLOCK)
    xb = x2.astype(BF16)
    r = jax.nn.sigmoid(jnp.dot(xb, wr_ref[...], preferred_element_type=F32) + br_ref[0])
    gi = jax.nn.sigmoid(jnp.dot(xb, wi_ref[...], preferred_element_type=F32) + bi_ref[0])
    log_a = (-RG_C) * r * sp_ref[0]
    a = jnp.exp(log_a)
    drive = jnp.sqrt(1.0 - a * a) * (gi * x2)
    a_ref[...] = a.reshape(ts, batch, C_BLOCK)
    d_ref[...] = drive.reshape(ts, batch, C_BLOCK)

    def step(i, h):
        idx = i + p * (ts - 1 - 2 * i)
        h = a_ref[idx] * h + d_ref[idx]
        hs_ref[idx] = h
        return h

    h_ref[...] = lax.fori_loop(0, ts, step, h_ref[...], unroll=8)
    base = pl.multiple_of(tb * ts, ts)

    @pl.when(p == 0)
    def _():
        hsf_ref[pl.ds(base, ts)] = hs_ref[...]

    @pl.when(p == 1)
    def _():
        hs = hsf_ref[pl.ds(base, ts)] + hs_ref[...]
        o_ref[...] = (hs.reshape(ts * batch, C_BLOCK) * _gelu_tanh(y_ref[...])).astype(o_ref.dtype)


def _rglru(u, conv_w, conv_b, w_r, b_r, w_i, b_i, a_param, seq, batch):
    ts = min(256, seq)
    n_t = seq // ts
    nc = C_GATE_BLOCKS
    u3 = u.reshape(seq, batch, 2 * C_WIDTH)
    sp = jax.nn.softplus(-a_param.astype(F32)).reshape(2, nc, 1, C_BLOCK)
    tblk = lambda p, t: t + p * (n_t - 1 - 2 * t)
    pinned = lambda p, t: p * tblk(p, t) + (1 - p) * (n_t - 1)
    per_dir = lambda shape: pl.BlockSpec((None, None) + shape, lambda c, p, t: (p, c, 0, 0))
    kern = functools.partial(_rglru_kernel, ts=ts, n_t=n_t, batch=batch)
    return pl.pallas_call(
        kern,
        out_shape=jax.ShapeDtypeStruct((seq * batch, C_WIDTH), BF16),
        grid=(nc, 2, n_t),
        in_specs=[
            pl.BlockSpec((ts, batch, C_BLOCK), lambda c, p, t: (tblk(p, t), 0, nc + c)),
            pl.BlockSpec((2, batch, C_BLOCK),
                         lambda c, p, t: (jnp.maximum(tblk(p, t) * (ts // 2) - 1, 0), 0, nc + c)),
            pl.BlockSpec((1, batch, C_BLOCK),
                         lambda c, p, t: (jnp.minimum((tblk(p, t) + 1) * ts, seq - 1), 0, nc + c)),
            pl.BlockSpec((ts * batch, C_BLOCK), lambda c, p, t: (pinned(p, t), c)),
            pl.BlockSpec((C_CONV, C_BLOCK), lambda c, p, t: (0, c)),
            pl.BlockSpec((1, C_BLOCK), lambda c, p, t: (0, c)),
            per_dir((C_BLOCK, C_BLOCK)), per_dir((1, C_BLOCK)),
            per_dir((C_BLOCK, C_BLOCK)), per_dir((1, C_BLOCK)),
            per_dir((1, C_BLOCK)),
        ],
        out_specs=pl.BlockSpec((ts * batch, C_BLOCK), lambda c, p, t: (pinned(p, t), c)),
        scratch_shapes=[pltpu.VMEM((seq, batch, C_BLOCK), F32),
                        pltpu.VMEM((ts, batch, C_BLOCK), F32),
                        pltpu.VMEM((ts, batch, C_BLOCK), F32),
                        pltpu.VMEM((ts, batch, C_BLOCK), F32),
                        pltpu.VMEM((batch, C_BLOCK), F32)],
        compiler_params=_cparams(("parallel", "arbitrary", "arbitrary")),
        name="rglru_bidirectional",
    )(u3, u3, u3, u, conv_w, conv_b.reshape(1, C_WIDTH),
      w_r.astype(BF16), b_r.reshape(2, nc, 1, C_BLOCK), w_i.astype(BF16), b_i.reshape(2, nc, 1, C_BLOCK), sp)


def _route_kernel(h_ref, rw_ref, rb_ref, e1_ref, e2_ref, g1_ref, g2_ref, r1_ref, r2_ref, cnt_ref,
                  carry_ref, *, tb):
    @pl.when(pl.program_id(0) == 0)
    def _():
        carry_ref[...] = jnp.zeros_like(carry_ref)

    h = h_ref[...]
    h_hi = h.astype(BF16)
    h_lo = (h - h_hi.astype(F32)).astype(BF16)
    w = rw_ref[...]
    w_hi = w.astype(BF16)
    w_lo = (w - w_hi.astype(F32)).astype(BF16)
    lt = (jnp.dot(h_hi, w_hi, preferred_element_type=F32) + jnp.dot(h_lo, w_hi, preferred_element_type=F32)
          + jnp.dot(h_hi, w_lo, preferred_element_type=F32))
    logits = lt.T[0:N_EXPERTS, :] + rb_ref[...]
    mx = jnp.max(logits, axis=0, keepdims=True)
    ex = jnp.exp(logits - mx)
    probs = ex / jnp.sum(ex, axis=0, keepdims=True)
    eid = lax.broadcasted_iota(I32, (N_EXPERTS, tb), 0)
    gmax = jnp.max(probs.reshape(N_GROUPS, EXPERTS_PER_GROUP, tb), axis=1)
    gid = lax.broadcasted_iota(I32, (N_GROUPS, tb), 0)
    grp = jnp.min(jnp.where(gmax == jnp.max(gmax, axis=0, keepdims=True), gid, N_GROUPS),
                  axis=0, keepdims=True)
    sel = jnp.where((eid // EXPERTS_PER_GROUP) == grp, probs, -1.0)
    m1 = jnp.max(sel, axis=0, keepdims=True)
    e1 = jnp.min(jnp.where(sel == m1, eid, N_EXPERTS), axis=0, keepdims=True)
    sel2 = jnp.where(eid == e1, -1.0, sel)
    m2 = jnp.max(sel2, axis=0, keepdims=True)
    e2 = jnp.min(jnp.where(sel2 == m2, eid, N_EXPERTS), axis=0, keepdims=True)
    tot = m1 + m2
    e1_ref[...] = e1
    e2_ref[...] = e2
    g1_ref[...] = m1 / tot
    g2_ref[...] = m2 / tot
    oh1 = eid == e1
    oh2 = eid == e2
    oh = jnp.where(oh1 | oh2, 1.0, 0.0)
    before = jnp.where(lax.broadcasted_iota(I32, (tb, tb), 0) < lax.broadcasted_iota(I32, (tb, tb), 1),
                       1.0, 0.0).astype(BF16)
    base = carry_ref[...] + jnp.dot(oh.astype(BF16), before, preferred_element_type=F32)
    r1_ref[...] = jnp.sum(jnp.where(oh1, base, 0.0), axis=0, keepdims=True).astype(I32)
    r2_ref[...] = jnp.sum(jnp.where(oh2, base, 0.0), axis=0, keepdims=True).astype(I32)
    carry_ref[...] = carry_ref[...] + jnp.sum(oh, axis=1, keepdims=True)
    cnt_ref[...] = jnp.broadcast_to(carry_ref[...], cnt_ref.shape).astype(I32)


def _route(h, router_w, router_b, tb=512):
    t, d = h.shape
    tb = min(tb, t)
    lane = lambda dt: jax.ShapeDtypeStruct((1, t), dt)
    lspec = pl.BlockSpec((1, tb), lambda i: (0, i))
    outs = pl.pallas_call(
        functools.partial(_route_kernel, tb=tb),
        out_shape=(lane(I32), lane(I32), lane(F32), lane(F32), lane(I32), lane(I32),
                   jax.ShapeDtypeStruct((N_EXPERTS, 128), I32)),
        grid=(t // tb,),
        in_specs=[pl.BlockSpec((tb, d), lambda i: (i, 0)),
                  pl.BlockSpec((d, 128), lambda i: (0, 0)),
                  pl.BlockSpec((N_EXPERTS, 1), lambda i: (0, 0))],
        out_specs=(lspec, lspec, lspec, lspec, lspec, lspec,
                   pl.BlockSpec((N_EXPERTS, 128), lambda i: (0, 0))),
        scratch_shapes=[pltpu.VMEM((N_EXPERTS, 1), F32)],
        compiler_params=_cparams(("arbitrary",)),
        name="moe_route",
    )(h, jnp.pad(router_w.astype(F32), ((0, 0), (0, 128 - N_EXPERTS))),
      router_b.reshape(N_EXPERTS, 1).astype(F32))
    e1, e2, g1, g2, r1, r2, cnt = outs
    return e1[0], e2[0], g1[0], g2[0], r1[0], r2[0], cnt[:, 0]


def _row_copy(src_ref, src_row, dst_ref, dst_row, sem):
    return pltpu.make_async_copy(src_ref.at[pl.ds(src_row, 1)], dst_ref.at[pl.ds(dst_row, 1)], sem)


def _dispatch_kernel(d1_ref, d2_ref, pad0_ref, padn_ref, h_ref, xs_ref, zero_ref, sems, *, tb):
    t0 = pl.program_id(0) * tb

    @pl.when(pl.program_id(0) == 0)
    def _():
        zero_ref[...] = jnp.zeros_like(zero_ref)

        def fill(wait):
            def segment(e, carry):
                def body(r, c):
                    cp = _row_copy(zero_ref, 0, xs_ref, pad0_ref[e] + r, sems.at[2])
                    cp.wait() if wait else cp.start()
                    return c
                return lax.fori_loop(0, padn_ref[e], body, carry)
            lax.fori_loop(0, N_EXPERTS + 1, segment, 0)

        fill(False)
        fill(True)

    def start(r, carry):
        _row_copy(h_ref, r, xs_ref, d1_ref[t0 + r], sems.at[0]).start()
        _row_copy(h_ref, r, xs_ref, d2_ref[t0 + r], sems.at[1]).start()
        return carry

    def wait(r, carry):
        _row_copy(h_ref, r, xs_ref, d1_ref[t0 + r], sems.at[0]).wait()
        _row_copy(h_ref, r, xs_ref, d2_ref[t0 + r], sems.at[1]).wait()
        return carry

    lax.fori_loop(0, tb, start, 0, unroll=4)
    lax.fori_loop(0, tb, wait, 0, unroll=4)


def _dispatch(h, dest1, dest2, pad_start, pad_count, n_rows, tb=128):
    t, d = h.shape
    tb = min(tb, t)
    return pl.pallas_call(
        functools.partial(_dispatch_kernel, tb=tb),
        out_shape=jax.ShapeDtypeStruct((n_rows, d), F32),
        grid_spec=pltpu.PrefetchScalarGridSpec(
            num_scalar_prefetch=4,
            grid=(t // tb,),
            in_specs=[pl.BlockSpec((tb, d), lambda i, *_: (i, 0))],
            out_specs=pl.BlockSpec(memory_space=pl.ANY),
            scratch_shapes=[pltpu.VMEM((8, d), F32), pltpu.SemaphoreType.DMA((3,))],
        ),
        compiler_params=_cparams(("arbitrary",)),
        name="moe_dispatch",
    )(dest1, dest2, pad_start, pad_count, h)


def _expert_kernel(be_ref, nu_ref, x_ref, wg_ref, wu_ref, wd_ref, o_ref):
    b = pl.program_id(0)

    @pl.when(b < nu_ref[0])
    def _():
        x = x_ref[...].astype(BF16)
        hg = jnp.dot(x, wg_ref[...], preferred_element_type=F32)
        hu = jnp.dot(x, wu_ref[...], preferred_element_type=F32)
        hid = (hg * jax.nn.sigmoid(hg) * hu).astype(BF16)
        o_ref[...] = jnp.dot(hid, wd_ref[...], preferred_element_type=F32)

    @pl.when(b >= nu_ref[0])
    def _():
        o_ref[...] = jnp.zeros_like(o_ref)


def _experts(xs, block_e, n_used, w_gate, w_up, w_down, layer):
    p, d = xs.shape
    nblk = p // MOE_BLOCK
    de = w_gate.shape[-1]
    xrow = lambda b, be, nu: (jnp.minimum(b, jnp.maximum(nu[0] - 1, 0)), 0)
    wsel = lambda b, be, nu: (layer, be[b], 0, 0)
    return pl.pallas_call(
        _expert_kernel,
        out_shape=jax.ShapeDtypeStruct((p, d), F32),
        grid_spec=pltpu.PrefetchScalarGridSpec(
            num_scalar_prefetch=2,
            grid=(nblk,),
            in_specs=[pl.BlockSpec((MOE_BLOCK, d), xrow),
                      pl.BlockSpec((None, None, d, de), wsel),
                      pl.BlockSpec((None, None, d, de), wsel),
                      pl.BlockSpec((None, None, de, d), wsel)],
            out_specs=pl.BlockSpec((MOE_BLOCK, d), lambda b, be, nu: (b, 0)),
        ),
        compiler_params=_cparams(("arbitrary",)),
        name="moe_experts",
    )(block_e, n_used, xs, w_gate, w_up, w_down)


def _combine_ln_kernel(d1_ref, d2_ref, yb_ref, h_ref, g1_ref, g2_ref, lg_ref, lb_ref, o_ref, obf_ref,
                       y1_ref, y2_ref, sems, *, tb):
    i = pl.program_id(0)
    slot = i % 2

    def copies(blk, s, r):
        t = blk * tb + r
        return (_row_copy(yb_ref, d1_ref[t], y1_ref.at[s], r, sems.at[s, 0]),
                _row_copy(yb_ref, d2_ref[t], y2_ref.at[s], r, sems.at[s, 1]))

    def gather(blk, s, wait):
        def body(r, carry):
            for c in copies(blk, s, r):
                c.wait() if wait else c.start()
            return carry
        lax.fori_loop(0, tb, body, 0, unroll=4)

    @pl.when(i == 0)
    def _():
        gather(0, 0, False)

    @pl.when(i + 1 < pl.num_programs(0))
    def _():
        gather(i + 1, 1 - slot, False)

    gather(i, slot, True)
    y = y1_ref[slot] * g1_ref[...] + y2_ref[slot] * g2_ref[...]
    out = _layer_norm_rows(ALPHA * h_ref[...] + y, lg_ref[...], lb_ref[...])
    o_ref[...] = out
    obf_ref[...] = out.astype(BF16)


def _combine_ln(yb, h, dest1, dest2, g1, g2, ln_g, ln_b, tb=128):
    t, d = h.shape
    tb = min(tb, t)
    row = pl.BlockSpec((tb, d), lambda i, d1, d2: (i, 0))
    colv = pl.BlockSpec((tb, 1), lambda i, d1, d2: (i, 0))
    vec = pl.BlockSpec((1, d), lambda i, d1, d2: (0, 0))
    return pl.pallas_call(
        functools.partial(_combine_ln_kernel, tb=tb),
        out_shape=(jax.ShapeDtypeStruct((t, d), F32), jax.ShapeDtypeStruct((t, d), BF16)),
        grid_spec=pltpu.PrefetchScalarGridSpec(
            num_scalar_prefetch=2,
            grid=(t // tb,),
            in_specs=[pl.BlockSpec(memory_space=pl.ANY), row, colv, colv, vec, vec],
            out_specs=(row, row),
            scratch_shapes=[pltpu.VMEM((2, tb, d), F32), pltpu.VMEM((2, tb, d), F32),
                            pltpu.SemaphoreType.DMA((2, 2))],
        ),
        compiler_params=_cparams(("arbitrary",)),
        name="moe_combine_layernorm",
    )(dest1, dest2, yb, h, g1.reshape(t, 1), g2.reshape(t, 1), ln_g.reshape(1, d), ln_b.reshape(1, d))


def _moe_ln(h, router_w, router_b, w_gate, w_up, w_down, layer, ln_g, ln_b):
    t, d = h.shape
    e1, e2, g1, g2, r1, r2, counts = _route(h, router_w, router_b)
    padded = (counts + MOE_BLOCK - 1) // MOE_BLOCK * MOE_BLOCK
    pends = jnp.cumsum(padded)
    pstarts = pends - padded
    dest1 = pstarts[e1] + r1
    dest2 = pstarts[e2] + r2
    nblk = -(-(t * TOP_K) // MOE_BLOCK) + N_EXPERTS
    n_used = (pends[-1] // MOE_BLOCK).astype(I32)
    blk_row = jnp.minimum(jnp.arange(nblk, dtype=I32), jnp.maximum(n_used - 1, 0)) * MOE_BLOCK
    block_e = jnp.minimum(jnp.sum(pends[None, :] <= blk_row[:, None], axis=1), N_EXPERTS - 1).astype(I32)
    n_rows = nblk * MOE_BLOCK
    pad_start = jnp.concatenate([pstarts + counts, pends[-1:]]).astype(I32)
    pad_count = jnp.concatenate([padded - counts, n_rows - pends[-1:]]).astype(I32)
    xs = _dispatch(h, dest1, dest2, pad_start, pad_count, n_rows)
    yb = _experts(xs, block_e, n_used.reshape(1), w_gate, w_up, w_down, layer)
    return _combine_ln(yb, h, dest1, dest2, g1, g2, ln_g, ln_b)


def _swap_token_order(a, outer, inner):
    return jnp.transpose(a.reshape(outer, inner, a.shape[-1]), (1, 0, 2)).reshape(outer * inner, a.shape[-1])


def kernel(x, ab_w_in, ab_w_out, hgrn_lower_bounds, hgrn_norm_g, attn_sink, rel_bias_table, c_w_in, c_conv_w, c_conv_b, c_w_rgate, c_b_rgate, c_w_igate, c_b_igate, c_a_param, c_w_out, ln_g, ln_b, router_w, router_b, moe_w_gate, moe_w_up, moe_w_down):
    batch, seq, d = x.shape
    t = batch * seq
    lb_all = jnp.cumsum(jax.nn.softmax(hgrn_lower_bounds.astype(F32), axis=0), axis=0)
    h = x.reshape(t, d)
    h_bf = h.astype(BF16)
    wg_bf, wu_bf, wd_bf = moe_w_gate.astype(BF16), moe_w_up.astype(BF16), moe_w_down.astype(BF16)
    batch_major = True
    for l in range(DEPTH):
        j = l // 2
        if (l % 2 == 0) != batch_major:
            outer, inner = (batch, seq) if batch_major else (seq, batch)
            h, h_bf = _swap_token_order(h, outer, inner), _swap_token_order(h_bf, outer, inner)
            batch_major = not batch_major
        if l % 2 == 0:
            u = _matmul(h_bf, ab_w_in[j].astype(BF16), F32, 1024, 1024)
            o_a = _hgrn(u, lb_all[l], hgrn_norm_g[j], seq, batch)
            o_b = _window_attention(u, attn_sink[j], rel_bias_table, seq, batch)
            mix = _matmul2(o_a, o_b, ab_w_out[j].astype(BF16), F32, 1024, 1024)
        else:
            u = _matmul(h_bf, c_w_in[j].astype(BF16), F32, 1024, 1024)
            z = _rglru(u, c_conv_w[j], c_conv_b[j], c_w_rgate[j], c_b_rgate[j], c_w_igate[j], c_b_igate[j],
                       c_a_param[j], seq, batch)
            mix = _matmul(z, c_w_out[j].astype(BF16), F32, 1024, 1024)
        h, _ = _res_ln(h, mix, ln_g[l, 0], ln_b[l, 0])
        h, h_bf = _moe_ln(h, router_w, router_b, wg_bf, wu_bf, wd_bf, l, ln_g[l, 1], ln_b[l, 1])
    if not batch_major:
        h = _swap_token_order(h, seq, batch)
    return h.reshape(batch, seq, d)
```
